```python
import math
import jax, jax.numpy as jnp
from jax import lax
import numpy as np

D_MODEL = 1024
BATCH = 16
SEQ = 2048
DEPTH = 2

CHUNK = 64
Q_BLOCK = 128
N_A = DEPTH // 2
N_B = DEPTH - N_A
ALPHA = (2.0 * DEPTH) ** 0.25
BETA = (8.0 * DEPTH) ** -0.25
A_HEAD_DIM = 64
A_HEADS = D_MODEL // A_HEAD_DIM
LORA_W = 64
LORA_A = 64
LORA_G = 128
GN_EPS = 64e-5
B_HEAD_DIM = 64
B_HEADS = D_MODEL // (2 * B_HEAD_DIM)
SUBLN_EPS = 1e-5
REL_BUCKETS = 32
REL_MAX_DIST = 128
D_FF = 4 * D_MODEL
LN_EPS = 1e-5

kernel_name = "rwkv7_diffattn_yoco_deepnorm"


def _layer_norm(x, g, b, eps):
    xf = x.astype(jnp.float32)
    mu = jnp.mean(xf, axis=-1, keepdims=True)
    var = jnp.mean(jnp.square(xf - mu), axis=-1, keepdims=True)
    return ((xf - mu) * lax.rsqrt(var + eps) * g + b).astype(x.dtype)


def _token_shift(x):
    return jnp.pad(x, ((0, 0), (1, 0), (0, 0)))[:, :-1, :]


def _wkv7_scan(r, decay, k, v, a_in, b_in):
    bsz, _, nh, n = r.shape

    def step(S, inp):
        r_t, w_t, k_t, v_t, a_t, b_t = inp
        sa = jnp.einsum('bhij,bhj->bhi', S, a_t)
        S = (S * w_t[:, :, None, :] + sa[..., None] * b_t[:, :, None, :]
             + v_t[..., None] * k_t[:, :, None, :])
        y_t = jnp.einsum('bhij,bhj->bhi', S, r_t)
        return S, y_t

    xs = tuple(jnp.moveaxis(t, 1, 0) for t in (r, decay, k, v, a_in, b_in))
    S0 = jnp.zeros((bsz, nh, n, n), jnp.float32)
    _, ys = lax.scan(step, S0, xs)
    return jnp.moveaxis(ys, 0, 1)


def _rwkv7_time_mix(x, mu, w_r, w_k, w_v, w_o, w0, w1, w2, a0, a1, a2, g1, g2,
                    k_k, k_a, r_k, lnx_g, lnx_b):
    bsz, t, c = x.shape
    xx = _token_shift(x) - x
    xr = x + xx * mu[0]
    xw = x + xx * mu[1]
    xk = x + xx * mu[2]
    xv = x + xx * mu[3]
    xa = x + xx * mu[4]
    xg = x + xx * mu[5]
    r = xr @ w_r
    w = -jax.nn.softplus(-(w0 + jnp.tanh(xw @ w1) @ w2)) - 0.5
    k = xk @ w_k
    v = xv @ w_v
    a = jax.nn.sigmoid(a0 + (xa @ a1) @ a2)
    g = jax.nn.sigmoid(xg @ g1) @ g2

    def heads(z):
        return z.reshape(bsz, t, A_HEADS, A_HEAD_DIM).astype(jnp.float32)

    r, w, k, v, a = heads(r), heads(w), heads(k), heads(v), heads(a)
    kk = k * k_k.reshape(A_HEADS, A_HEAD_DIM).astype(jnp.float32)
    kk = kk / jnp.maximum(jnp.sqrt(jnp.sum(kk * kk, axis=-1, keepdims=True)), 1e-12)
    k = k * (1.0 + (a - 1.0) * k_a.reshape(A_HEADS, A_HEAD_DIM).astype(jnp.float32))
    decay = jnp.exp(-jnp.exp(w))
    y = _wkv7_scan(r, decay, k, v, -kk, kk * a)
    y = _layer_norm(y, lnx_g.reshape(A_HEADS, A_HEAD_DIM), lnx_b.reshape(A_HEADS, A_HEAD_DIM), GN_EPS)
    y = y + jnp.sum(r * k * r_k.astype(jnp.float32), axis=-1, keepdims=True) * v
    y = y.reshape(bsz, t, c).astype(x.dtype) * g
    return y @ w_o


def _t5_bucket(rel):
    nb = REL_BUCKETS // 2
    max_exact = nb // 2
    ret = jnp.where(rel > 0, nb, 0)
    n = jnp.abs(rel)
    nf = jnp.maximum(n, 1).astype(jnp.float32)
    large = max_exact + (jnp.log(nf / max_exact) / math.log(REL_MAX_DIST / max_exact)
                         * (nb - max_exact)).astype(jnp.int32)
    large = jnp.minimum(large, nb - 1)
    return ret + jnp.where(n < max_exact, n, large)


def _diff_attention(x, k_sh, v_sh, w_q, lam, subln_g, w_o, rel_bias, lambda_init):
    bsz, t, c = x.shape
    scale = B_HEAD_DIM ** -0.5
    q = (x @ w_q).reshape(bsz, t, 2 * B_HEADS, B_HEAD_DIM)
    lamf = lam.astype(jnp.float32)
    lam_full = (jnp.exp(jnp.sum(lamf[0] * lamf[1])) - jnp.exp(jnp.sum(lamf[2] * lamf[3]))
                + lambda_init)
    outs = []
    for blk in range(t // Q_BLOCK):
        q0 = blk * Q_BLOCK
        k_end = q0 + Q_BLOCK
        qb = q[:, q0:k_end]
        kb = k_sh[:, :k_end]
        vb = v_sh[:, :k_end].astype(jnp.float32)
        s = jnp.einsum('bqhd,bkhd->bhqk', qb, kb).astype(jnp.float32) * scale
        s = s.reshape(bsz, B_HEADS, 2, Q_BLOCK, k_end)
        q_pos = jnp.arange(q0, k_end, dtype=jnp.int32)
        k_pos = jnp.arange(k_end, dtype=jnp.int32)
        bias = rel_bias[_t5_bucket(k_pos[None, :] - q_pos[:, None])]
        bias = jnp.transpose(bias, (2, 0, 1)).astype(jnp.float32)
        allowed = (k_pos[None, :] // CHUNK) <= (q_pos[:, None] // CHUNK)
        s = jnp.where(allowed, s + bias[None, :, None], -jnp.inf)
        p = jax.nn.softmax(s, axis=-1)
        attn = p[:, :, 0] - lam_full * p[:, :, 1]
        o = jnp.einsum('bhqk,bkhe->bqhe', attn, vb)
        o = o * lax.rsqrt(jnp.mean(o * o, axis=-1, keepdims=True) + SUBLN_EPS)
        o = o * subln_g.astype(jnp.float32) * (1.0 - lambda_init)
        outs.append(o.reshape(bsz, Q_BLOCK, c).astype(x.dtype))
    return jnp.concatenate(outs, axis=1) @ w_o


def _sqrelu_mlp(x, w1, w2):
    return jnp.square(jax.nn.relu(x @ w1)) @ w2


def setup_inputs(seed: int = 0) -> dict:
    key = jax.random.key(seed)
    ks = iter(jax.random.split(key, 40))
    d = D_MODEL
    f32 = jnp.float32

    def nrm(shape, s):
        return jax.random.normal(next(ks), shape, f32) * s

    x = jax.random.normal(next(ks), (BATCH, SEQ, d), f32)
    a_mu = jax.random.uniform(next(ks), (N_A, 6, d), f32)
    a_w_r = nrm((N_A, d, d), d ** -0.5)
    a_w_k = nrm((N_A, d, d), d ** -0.5)
    a_w_v = nrm((N_A, d, d), BETA * d ** -0.5)
    a_w_o = nrm((N_A, d, d), BETA * d ** -0.5)
    a_w0 = jax.random.uniform(next(ks), (N_A, d), f32, -6.0, 1.0)
    a_w1 = nrm((N_A, d, LORA_W), d ** -0.5)
    a_w2 = nrm((N_A, LORA_W, d), 0.5 * LORA_W ** -0.5)
    a_a0 = nrm((N_A, d), 0.1)
    a_a1 = nrm((N_A, d, LORA_A), d ** -0.5)
    a_a2 = nrm((N_A, LORA_A, d), 0.5 * LORA_A ** -0.5)
    a_g1 = nrm((N_A, d, LORA_G), d ** -0.5)
    a_g2 = nrm((N_A, LORA_G, d), LORA_G ** -0.5)
    a_k_k = 0.85 + nrm((N_A, d), 0.05)
    a_k_a = 1.0 + nrm((N_A, d), 0.05)
    a_r_k = nrm((N_A, A_HEADS, A_HEAD_DIM), 0.1)
    a_lnx_g = 1.0 + nrm((N_A, d), 0.05)
    a_lnx_b = nrm((N_A, d), 0.01)
    b_w_kv = jnp.concatenate([nrm((d, d), d ** -0.5), nrm((d, d), BETA * d ** -0.5)], axis=1)
    b_w_q = nrm((N_B, d, d), d ** -0.5)
    b_lam = nrm((N_B, 4, B_HEAD_DIM), 0.1)
    b_subln_g = 1.0 + nrm((N_B, 2 * B_HEAD_DIM), 0.05)
    b_w_o = nrm((N_B, d, d), BETA * d ** -0.5)
    rel_bias = nrm((REL_BUCKETS, B_HEADS), 0.5)
    mlp_w1 = nrm((DEPTH, d, D_FF), BETA * d ** -0.5)
    mlp_w2 = nrm((DEPTH, D_FF, d), BETA * D_FF ** -0.5)
    ln_g = 1.0 + nrm((DEPTH, 2, d), 0.05)
    ln_b = nrm((DEPTH, 2, d), 0.01)
    return {"x": x, "a_mu": a_mu, "a_w_r": a_w_r, "a_w_k": a_w_k, "a_w_v": a_w_v,
            "a_w_o": a_w_o, "a_w0": a_w0, "a_w1": a_w1, "a_w2": a_w2, "a_a0": a_a0,
            "a_a1": a_a1, "a_a2": a_a2, "a_g1": a_g1, "a_g2": a_g2, "a_k_k": a_k_k,
            "a_k_a": a_k_a, "a_r_k": a_r_k, "a_lnx_g": a_lnx_g, "a_lnx_b": a_lnx_b,
            "b_w_kv": b_w_kv, "b_w_q": b_w_q, "b_lam": b_lam, "b_subln_g": b_subln_g,
            "b_w_o": b_w_o, "rel_bias": rel_bias, "mlp_w1": mlp_w1, "mlp_w2": mlp_w2,
            "ln_g": ln_g, "ln_b": ln_b}


def reference(x, a_mu, a_w_r, a_w_k, a_w_v, a_w_o, a_w0, a_w1, a_w2, a_a0, a_a1, a_a2,
              a_g1, a_g2, a_k_k, a_k_a, a_r_k, a_lnx_g, a_lnx_b, b_w_kv, b_w_q, b_lam,
              b_subln_g, b_w_o, rel_bias, mlp_w1, mlp_w2, ln_g, ln_b):
    bsz, t, d = x.shape
    k_sh = None
    v_sh = None
    for l in range(DEPTH):
        if l < N_A:
            i = l
            h = _rwkv7_time_mix(x, a_mu[i], a_w_r[i], a_w_k[i], a_w_v[i], a_w_o[i],
                                a_w0[i], a_w1[i], a_w2[i], a_a0[i], a_a1[i], a_a2[i],
                                a_g1[i], a_g2[i], a_k_k[i], a_k_a[i], a_r_k[i],
                                a_lnx_g[i], a_lnx_b[i])
        else:
            if l == N_A:
                kv = x @ b_w_kv
                k_sh = kv[..., :d].reshape(bsz, t, 2 * B_HEADS, B_HEAD_DIM)
                v_sh = kv[..., d:].reshape(bsz, t, B_HEADS, 2 * B_HEAD_DIM)
            j = l - N_A
            lambda_init = 0.8 - 0.6 * math.exp(-0.3 * l)
            h = _diff_attention(x, k_sh, v_sh, b_w_q[j], b_lam[j], b_subln_g[j], b_w_o[j],
                                rel_bias, lambda_init)
        x = _layer_norm(ALPHA * x + h, ln_g[l, 0], ln_b[l, 0], LN_EPS)
        x = _layer_norm(ALPHA * x + _sqrelu_mlp(x, mlp_w1[l], mlp_w2[l]), ln_g[l, 1], ln_b[l, 1], LN_EPS)
    return x
```

```python
import functools
import math

import numpy as np
import jax
import jax.numpy as jnp
from jax import lax
from jax.experimental import pallas as pl
from jax.experimental.pallas import tpu as pltpu

F32 = jnp.float32
BF16 = jnp.bfloat16

DEPTH = 2
ALPHA = (2.0 * DEPTH) ** 0.25
HEAD = 64
GN_EPS = 64e-5
SUBLN_EPS = 1e-5
LN_EPS = 1e-5
REL_BUCKETS = 32
REL_MAX_DIST = 128
ATT_CHUNK = 64
Q_BLOCK = 128

LANES = 128
MXU_DIM = 256
WKV_L = 64
PACK = MXU_DIM // HEAD
KEY_TILE = 256
NEG_BIG = -1e30
VMEM_LIMIT = 56 * 1024 * 1024


def _dot(a, b):
    return jnp.dot(a.astype(BF16), b.astype(BF16), preferred_element_type=F32)


def _dot_nt(a, b):
    return lax.dot_general(a.astype(BF16), b.astype(BF16), (((1,), (1,)), ((), ())),
                           preferred_element_type=F32)


def _dot_tn(a, b):
    return lax.dot_general(a.astype(BF16), b.astype(BF16), (((0,), (0,)), ((), ())),
                           preferred_element_type=F32)


def _iota(shape, dim):
    return lax.broadcasted_iota(jnp.int32, shape, dim)


def _head_block_ones():
    r = _iota((MXU_DIM, MXU_DIM), 0) // HEAD
    c = _iota((MXU_DIM, MXU_DIM), 1) // HEAD
    return jnp.where(r == c, 1.0, 0.0).astype(BF16)


def _head_sum_bcast(x, ones_bd):
    d = x.shape[1]
    parts = [_dot(x[:, p:p + MXU_DIM], ones_bd) for p in range(0, d, MXU_DIM)]
    return jnp.concatenate(parts, axis=1)


def _layer_norm(x, g, b, eps):
    mu = jnp.mean(x, axis=-1, keepdims=True)
    xc = x - mu
    var = jnp.mean(xc * xc, axis=-1, keepdims=True)
    return xc * lax.rsqrt(var + eps) * g + b


def _rwkv_proj_kernel(x_ref, xp_ref, mu_ref, wr_ref, wk_ref, wv_ref, w0_ref, w1_ref, w2_ref,
                      a0_ref, a1_ref, a2_ref, g1_ref, g2_ref, kk_ref, ka_ref, rk_ref,
                      rt_ref, kt_ref, bt_ref, at_ref, v_ref, bonus_ref, g_ref, gl_ref):
    i = pl.program_id(1)
    x = x_ref[...]
    tm, d = x.shape
    prev_row = jnp.where(i == 0, 0.0, xp_ref[7:8, :])
    row = _iota((tm, d), 0)
    xprev = jnp.where(row == 0, prev_row, pltpu.roll(x, 1, 0))
    xx = xprev - x
    mu = mu_ref[...]

    def mix(j):
        return (x + xx * mu[j:j + 1, :]).astype(BF16)

    r = _dot(mix(0), wr_ref[...])
    wraw = w0_ref[...] + _dot(jnp.tanh(_dot(mix(1), w1_ref[...])), w2_ref[...])
    k = _dot(mix(2), wk_ref[...])
    v = _dot(mix(3), wv_ref[...])
    a = jax.nn.sigmoid(a0_ref[...] + _dot(_dot(mix(4), a1_ref[...]), a2_ref[...]))
    g = _dot(jax.nn.sigmoid(_dot(mix(5), g1_ref[...])), g2_ref[...])

    w = jnp.minimum(wraw, 0.0) - jnp.log1p(jnp.exp(-jnp.abs(wraw))) - 0.5
    ld = -jnp.exp(w)

    ones_bd = _head_block_ones()
    kk = k * kk_ref[...]
    kk = kk / jnp.maximum(jnp.sqrt(_head_sum_bcast(kk * kk, ones_bd)), 1e-12)
    k = k * (1.0 + (a - 1.0) * ka_ref[...])
    bonus = _head_sum_bcast(r * k * rk_ref[...], ones_bd) * v

    tr = _iota((MXU_DIM, MXU_DIM), 0)
    tc = _iota((MXU_DIM, MXU_DIM), 1)
    tri = jnp.where((tr // WKV_L == tc // WKV_L) & (tc <= tr), 1.0, 0.0).astype(BF16)
    ld_hi = ld.astype(BF16)
    ld_lo = (ld - ld_hi.astype(F32)).astype(BF16)
    cs = jnp.concatenate(
        [jnp.dot(tri, ld_hi[p:p + MXU_DIM, :], preferred_element_type=F32)
         + jnp.dot(tri, ld_lo[p:p + MXU_DIM, :], preferred_element_type=F32)
         for p in range(0, tm, MXU_DIM)], axis=0)

    gam = jnp.exp(cs)
    igam = jnp.exp(-cs)
    rt_ref[...] = (r * gam).astype(BF16)
    kt_ref[...] = (k * igam).astype(BF16)
    bt_ref[...] = (kk * a * igam).astype(BF16)
    at_ref[...] = (-kk * jnp.exp(cs - ld)).astype(BF16)
    v_ref[...] = v.astype(BF16)
    bonus_ref[...] = bonus.astype(BF16)
    g_ref[...] = g.astype(BF16)
    gl_ref[...] = jnp.exp(cs.reshape(tm // WKV_L, WKV_L, d)[:, WKV_L - 1, :])


def _rwkv_proj(x, p, tm=512):
    bsz, t, d = x.shape
    nt = t // tm
    row_spec = pl.BlockSpec((None, tm, d), lambda b, i: (b, i, 0))
    prev_spec = pl.BlockSpec((None, 8, d), lambda b, i: (b, jnp.maximum(i * (tm // 8) - 1, 0), 0))

    def full(arr):
        nd = arr.ndim
        return pl.BlockSpec(arr.shape, lambda b, i: (0,) * nd, pipeline_mode=pl.Buffered(1))

    weights = [p["mu"], p["w_r"], p["w_k"], p["w_v"], p["w0"], p["w1"], p["w2"], p["a0"], p["a1"], p["a2"],
               p["g1"], p["g2"], p["k_k"], p["k_a"], p["r_k"]]
    out_bf = jax.ShapeDtypeStruct((bsz, t, d), BF16)
    out_shape = [out_bf] * 7 + [jax.ShapeDtypeStruct((bsz, t // WKV_L, d), F32)]
    out_specs = [row_spec] * 7 + [pl.BlockSpec((None, tm // WKV_L, d), lambda b, i: (b, i, 0))]
    return pl.pallas_call(
        _rwkv_proj_kernel,
        grid=(bsz, nt),
        in_specs=[row_spec, prev_spec] + [full(w) for w in weights],
        out_specs=out_specs,
        out_shape=out_shape,
        compiler_params=pltpu.CompilerParams(dimension_semantics=("parallel", "parallel"),
                                             vmem_limit_bytes=VMEM_LIMIT),
        name="rwkv_proj",
    )(x, x, *weights)


def _wkv_kernel(rt_ref, kt_ref, bt_ref, at_ref, v_ref, gl_ref, y_ref, st_ref):
    t = rt_ref.shape[0]
    L, W = WKV_L, MXU_DIM
    row = _iota((L, W), 0)
    sidx = _iota((L, W), 1) % L
    strict = sidx < row
    incl = sidx <= row
    eye = sidx == row
    r2 = _iota((W, W), 0)
    c2 = _iota((W, W), 1)
    bd = (r2 // L) == (c2 // L)
    diag = r2 == c2

    def stack(xb):
        return jnp.where(bd, jnp.concatenate([xb] * PACK, axis=0), jnp.zeros((), xb.dtype))

    st_ref[...] = jnp.zeros_like(st_ref)

    def body(c, carry):
        sl = pl.ds(pl.multiple_of(c * L, L), L)
        rt = rt_ref[sl, :]
        kt = kt_ref[sl, :]
        bt = bt_ref[sl, :]
        at = at_ref[sl, :]
        v = v_ref[sl, :]
        gl = gl_ref[pl.ds(c, 1), :]

        gmat = _dot_nt(jnp.concatenate([at, rt], axis=0),
                       jnp.concatenate([stack(bt), stack(kt)], axis=0))
        ab = gmat[:L, :W]
        ak = gmat[:L, W:]
        rb = gmat[L:, :W]
        rk = gmat[L:, W:]

        n1 = jnp.where(strict, ab, 0.0)
        tm_ = jnp.where(eye, 1.0, 0.0) + n1
        cur = n1.astype(BF16)
        cur = _dot(cur, stack(cur)).astype(BF16)
        for _ in range(4):
            res = _dot(jnp.concatenate([cur, tm_.astype(BF16)], axis=0), stack(cur))
            tm_ = tm_ + res[L:]
            cur = res[:L].astype(BF16)
        tm_ = tm_ + _dot(tm_, stack(cur))

        sv = stack(v)
        res = _dot(jnp.concatenate([jnp.where(strict, ak, 0.0), jnp.where(incl, rk, 0.0)], axis=0), sv)
        x1 = res[:L]
        rkv = res[L:]
        uw = _dot(tm_, jnp.concatenate([stack(x1.astype(BF16)), stack(at)], axis=1))
        u0 = uw[:, :W].astype(BF16)
        wm = uw[:, W:].astype(BF16)
        yr = _dot(jnp.where(incl, rb, 0.0), jnp.concatenate([stack(u0), stack(wm)], axis=1))
        y0 = yr[:, :W] + rkv
        rp = rt.astype(F32) + yr[:, W:]

        bh = (bt.astype(F32) * gl).astype(BF16)
        kh = (kt.astype(F32) * gl).astype(BF16)
        rhs2 = jnp.concatenate([jnp.concatenate([u0, wm], axis=1),
                                jnp.concatenate([v, jnp.zeros((L, W), BF16)], axis=1)], axis=0)
        pq = _dot_tn(jnp.concatenate([bh, kh], axis=0), rhs2)
        q = jnp.where(bd, pq[:, :W], 0.0)
        p = jnp.where(bd, pq[:, W:], 0.0) + jnp.where(diag, gl, 0.0)

        res = _dot(jnp.concatenate([p.astype(BF16), rp.astype(BF16)], axis=0), st_ref[...])
        st_ref[...] = res[:W] + q
        y_ref[sl, :] = (y0 + res[W:]).astype(y_ref.dtype)
        return carry

    lax.fori_loop(0, t // L, body, 0)


def _wkv(rt, kt, bt, at, v, gl):
    bsz, t, d = rt.shape
    npack = d // MXU_DIM
    seq_spec = pl.BlockSpec((None, t, MXU_DIM), lambda b, p: (b, 0, p))
    gl_spec = pl.BlockSpec((None, t // WKV_L, MXU_DIM), lambda b, p: (b, 0, p))
    return pl.pallas_call(
        _wkv_kernel,
        grid=(bsz, npack),
        in_specs=[seq_spec] * 5 + [gl_spec],
        out_specs=seq_spec,
        out_shape=jax.ShapeDtypeStruct((bsz, t, d), BF16),
        scratch_shapes=[pltpu.VMEM((MXU_DIM, MXU_DIM), F32)],
        compiler_params=pltpu.CompilerParams(dimension_semantics=("parallel", "parallel"),
                                             vmem_limit_bytes=VMEM_LIMIT),
        name="wkv7",
    )(rt, kt, bt, at, v, gl)


def _post_block_kernel(*refs, rwkv, project):
    it = iter(refs)
    x_ref = next(it)
    y_ref = next(it)
    if rwkv:
        bonus_ref, g_ref, lnxg_ref, lnxb_ref = next(it), next(it), next(it), next(it)
    wo_ref, ln_ref, w1_ref, w2_ref = next(it), next(it), next(it), next(it)
    if project:
        wq_ref, wkv_ref = next(it), next(it)
    out_ref = next(it)
    if project:
        q_ref, k_ref, v_ref = next(it), next(it), next(it)

    x = x_ref[...]
    d = x.shape[1]
    if rwkv:
        ones_bd = _head_block_ones()
        y = y_ref[...].astype(F32)
        mean = _head_sum_bcast(y, ones_bd) * (1.0 / HEAD)
        yc = y - mean
        var = _head_sum_bcast(yc * yc, ones_bd) * (1.0 / HEAD)
        yn = yc * lax.rsqrt(var + GN_EPS) * lnxg_ref[...] + lnxb_ref[...]
        mixed = ((yn + bonus_ref[...].astype(F32)) * g_ref[...].astype(F32)).astype(BF16)
    else:
        mixed = y_ref[...]
    ln = ln_ref[...]
    x1 = _layer_norm(ALPHA * x + _dot(mixed, wo_ref[...]), ln[0:1, :], ln[1:2, :], LN_EPS)
    hid = jnp.maximum(_dot(x1, w1_ref[...]), 0.0)
    x2 = _layer_norm(ALPHA * x1 + _dot(hid * hid, w2_ref[...]), ln[2:3, :], ln[3:4, :], LN_EPS)
    out_ref[...] = x2
    if project:
        x2b = x2.astype(BF16)
        q_ref[...] = (_dot(x2b, wq_ref[...]) * (HEAD ** -0.5)).astype(BF16)
        kv = _dot(x2b, wkv_ref[...])
        k_ref[...] = kv[:, :d].astype(BF16)
        v_ref[...] = kv[:, d:].astype(BF16)


def _post_block(x, y, rw, w_o, ln, w1, w2, proj, tm=256):
    bsz, t, d = x.shape
    rwkv = rw is not None
    project = proj is not None
    row_spec = pl.BlockSpec((None, tm, d), lambda b, i: (b, i, 0))

    def full(arr):
        nd = arr.ndim
        return pl.BlockSpec(arr.shape, lambda b, i: (0,) * nd, pipeline_mode=pl.Buffered(1))

    args = [x, y]
    specs = [row_spec, row_spec]
    if rwkv:
        args += [rw["bonus"], rw["g"], rw["lnx_g"], rw["lnx_b"]]
        specs += [row_spec, row_spec, full(rw["lnx_g"]), full(rw["lnx_b"])]
    args += [w_o, ln, w1, w2]
    specs += [full(w_o), full(ln), full(w1), full(w2)]
    out_shape = [jax.ShapeDtypeStruct((bsz, t, d), F32)]
    out_specs = [row_spec]
    if project:
        args += [proj["w_q"], proj["w_kv"]]
        specs += [full(proj["w_q"]), full(proj["w_kv"])]
        out_shape += [jax.ShapeDtypeStruct((bsz, t, d), BF16)] * 3
        out_specs += [row_spec] * 3
    return pl.pallas_call(
        functools.partial(_post_block_kernel, rwkv=rwkv, project=project),
        grid=(bsz, t // tm),
        in_specs=specs,
        out_specs=out_specs,
        out_shape=out_shape,
        compiler_params=pltpu.CompilerParams(dimension_semantics=("parallel", "parallel"),
                                             vmem_limit_bytes=VMEM_LIMIT),
        name="post_rwkv" if rwkv else "post_attn",
    )(*args)


def _t5_bucket_np(rel):
    nb = REL_BUCKETS // 2
    max_exact = nb // 2
    n = np.abs(rel)
    thresholds = [int(math.ceil(max_exact * (REL_MAX_DIST / max_exact) ** (m / (nb - max_exact)) - 1e-9))
                  for m in range(1, nb - max_exact)]
    large = max_exact + sum((n >= th).astype(np.int64) for th in thresholds)
    return np.where(rel > 0, nb, 0) + np.where(n < max_exact, n, np.minimum(large, nb - 1))


def _bias_bucket_tables():
    ql = np.arange(Q_BLOCK)[:, None]
    kl = np.arange(KEY_TILE)[None, :]
    out = np.zeros((4, Q_BLOCK, KEY_TILE), np.int32)
    for par in range(2):
        qpos = Q_BLOCK * par + ql
        out[2 * par + 0] = _t5_bucket_np(kl - KEY_TILE - qpos)
        allowed = (kl // ATT_CHUNK) <= (qpos // ATT_CHUNK)
        out[2 * par + 1] = np.where(allowed, _t5_bucket_np(kl - qpos), -1)
    return out


def _bias_table_kernel(bucket_ref, relb_ref, out_ref):
    h = pl.program_id(0)
    bucket = bucket_ref[...]
    acc = jnp.where(bucket < 0, NEG_BIG, 0.0).astype(F32)
    for b in range(REL_BUCKETS):
        acc = jnp.where(bucket == b, relb_ref[b, h], acc)
    out_ref[...] = acc


def _bias_tables(rel_bias):
    nh = rel_bias.shape[1]
    buckets = jnp.asarray(_bias_bucket_tables())
    return pl.pallas_call(
        _bias_table_kernel,
        grid=(nh,),
        in_specs=[pl.BlockSpec(buckets.shape, lambda h: (0, 0, 0)),
                  pl.BlockSpec(memory_space=pltpu.SMEM)],
        out_specs=pl.BlockSpec((None,) + buckets.shape, lambda h: (h, 0, 0, 0)),
        out_shape=jax.ShapeDtypeStruct((nh,) + buckets.shape, F32),
        name="t5_bias_tables",
    )(buckets, rel_bias)


def _diff_attn_kernel(q_ref, k_ref, v_ref, tab_ref, relb_ref, lam_ref, sg_ref, o_ref, *, lambda_init):
    h = pl.program_id(1)
    t = q_ref.shape[0]
    qb, kt_, dh = Q_BLOCK, KEY_TILE, HEAD
    c_far = relb_ref[REL_BUCKETS // 2 - 1, h]
    lam = lam_ref[...]
    lam_full = (jnp.exp(jnp.sum(lam[0:1, :] * lam[1:2, :], axis=-1, keepdims=True))
                - jnp.exp(jnp.sum(lam[2:3, :] * lam[3:4, :], axis=-1, keepdims=True)) + lambda_init)
    lane = _iota((qb, 2 * dh), 1)
    sg = sg_ref[...] * (1.0 - lambda_init)

    def q_block(qi, carry):
        qs = pl.ds(pl.multiple_of(qi * qb, qb), qb)
        q = q_ref[qs, :]
        zero = jnp.zeros((), q.dtype)
        qm = jnp.concatenate([jnp.where(lane < dh, q, zero), jnp.where(lane >= dh, q, zero)], axis=0)
        j_last = qi // 2
        par = qi % 2

        def tile(j, state, bias):
            m, l, acc = state
            ks = pl.ds(pl.multiple_of(j * kt_, kt_), kt_)
            s = _dot_nt(qm, k_ref[ks, :]) + bias
            m_new = jnp.maximum(m, jnp.max(s, axis=-1, keepdims=True))
            alpha = jnp.exp(m - m_new)
            pexp = jnp.exp(s - m_new)
            l = alpha * l + jnp.sum(pexp, axis=-1, keepdims=True)
            acc = alpha * acc + _dot(pexp, v_ref[ks, :])
            return m_new, l, acc

        state = (jnp.full((2 * qb, 1), NEG_BIG, F32), jnp.zeros((2 * qb, 1), F32),
                 jnp.zeros((2 * qb, 2 * dh), F32))
        state = lax.fori_loop(0, jnp.maximum(j_last - 1, 0), lambda j, st: tile(j, st, c_far), state)
        prev_tab = tab_ref[2 * par]
        prev_bias = jnp.where(j_last >= 1, prev_tab, NEG_BIG)
        state = tile(jnp.maximum(j_last - 1, 0), state, jnp.concatenate([prev_bias, prev_bias], axis=0))
        last_tab = tab_ref[2 * par + 1]
        m, l, acc = tile(j_last, state, jnp.concatenate([last_tab, last_tab], axis=0))
        o = acc[:qb] / l[:qb] - lam_full * (acc[qb:] / l[qb:])
        o = o * lax.rsqrt(jnp.mean(o * o, axis=-1, keepdims=True) + SUBLN_EPS)
        o_ref[qs, :] = (o * sg).astype(o_ref.dtype)
        return carry

    lax.fori_loop(0, t // qb, q_block, 0)


def _diff_attn(q, k, v, tabs, rel_bias, lam, subln_g, lambda_init):
    bsz, t, d = q.shape
    nh = d // (2 * HEAD)
    seq_spec = pl.BlockSpec((None, t, 2 * HEAD), lambda b, h: (b, 0, h))
    return pl.pallas_call(
        functools.partial(_diff_attn_kernel, lambda_init=lambda_init),
        grid=(bsz, nh),
        in_specs=[seq_spec, seq_spec, seq_spec,
                  pl.BlockSpec((None,) + tabs.shape[1:], lambda b, h: (h, 0, 0, 0)),
                  pl.BlockSpec(memory_space=pltpu.SMEM),
                  pl.BlockSpec(lam.shape, lambda b, h: (0, 0)),
                  pl.BlockSpec(subln_g.shape, lambda b, h: (0, 0))],
        out_specs=seq_spec,
        out_shape=jax.ShapeDtypeStruct((bsz, t, d), BF16),
        compiler_params=pltpu.CompilerParams(dimension_semantics=("parallel", "parallel"),
                                             vmem_limit_bytes=VMEM_LIMIT),
        name="diff_attn",
    )(q, k, v, tabs, rel_bias, lam, subln_g)


def kernel(x, a_mu, a_w_r, a_w_k, a_w_v, a_w_o, a_w0, a_w1, a_w2, a_a0, a_a1, a_a2, a_g1, a_g2, a_k_k, a_k_a,
           a_r_k, a_lnx_g, a_lnx_b, b_w_kv, b_w_q, b_lam, b_subln_g, b_w_o, rel_bias, mlp_w1, mlp_w2, ln_g, ln_b):
    bsz, t, d = x.shape
    assert d % MXU_DIM == 0 and t % 512 == 0
    bf = lambda w: w.astype(BF16)
    row = lambda w: w.reshape(1, d)

    p = dict(mu=a_mu[0], w_r=bf(a_w_r[0]), w_k=bf(a_w_k[0]), w_v=bf(a_w_v[0]), w0=row(a_w0[0]),
             w1=bf(a_w1[0]), w2=bf(a_w2[0]), a0=row(a_a0[0]), a1=bf(a_a1[0]), a2=bf(a_a2[0]),
             g1=bf(a_g1[0]), g2=bf(a_g2[0]), k_k=row(a_k_k[0]), k_a=row(a_k_a[0]), r_k=row(a_r_k[0]))
    rt, kt, bt, at, v, bonus, g, gl = _rwkv_proj(x, p)
    y = _wkv(rt, kt, bt, at, v, gl)
    ln0 = jnp.stack([ln_g[0, 0], ln_b[0, 0], ln_g[0, 1], ln_b[0, 1]])
    x, q, k_sh, v_sh = _post_block(
        x, y, dict(bonus=bonus, g=g, lnx_g=row(a_lnx_g[0]), lnx_b=row(a_lnx_b[0])),
        bf(a_w_o[0]), ln0, bf(mlp_w1[0]), bf(mlp_w2[0]), dict(w_q=bf(b_w_q[0]), w_kv=bf(b_w_kv)))

    lambda_init = 0.8 - 0.6 * math.exp(-0.3 * 1)
    tabs = _bias_tables(rel_bias)
    o = _diff_attn(q, k_sh, v_sh, tabs, rel_bias, b_lam[0], b_subln_g[0].reshape(1, 2 * HEAD), lambda_init)
    ln1 = jnp.stack([ln_g[1, 0], ln_b[1, 0], ln_g[1, 1], ln_b[1, 1]])
    (x,) = _post_block(x, o, None, bf(b_w_o[0]), ln1, bf(mlp_w1[1]), bf(mlp_w2[1]), None)
    return x
```

```python
import functools
import itertools
import math

import numpy as np
import jax
import jax.numpy as jnp
from jax import lax
from jax.experimental import pallas as pl
from jax.experimental.pallas import tpu as pltpu

F32 = jnp.float32
BF16 = jnp.bfloat16

DEPTH = 2
ALPHA = (2.0 * DEPTH) ** 0.25
HEAD = 64
GN_EPS = 64e-5
SUBLN_EPS = 1e-5
LN_EPS = 1e-5
REL_BUCKETS = 32
REL_MAX_DIST = 128
ATT_CHUNK = 64
Q_BLOCK = 128

LANES = 128
MXU_DIM = 256
WKV_L = 64
PACK = MXU_DIM // HEAD
KEY_TILE = 256
NEG_BIG = -1e30
VMEM_LIMIT = 56 * 1024 * 1024


def _dot(a, b):
    return jnp.dot(a.astype(BF16), b.astype(BF16), preferred_element_type=F32)


def _dot_nt(a, b):
    return lax.dot_general(a.astype(BF16), b.astype(BF16), (((1,), (1,)), ((), ())),
                           preferred_element_type=F32)


def _dot_tn(a, b):
    return lax.dot_general(a.astype(BF16), b.astype(BF16), (((0,), (0,)), ((), ())),
                           preferred_element_type=F32)


def _iota(shape, dim):
    return lax.broadcasted_iota(jnp.int32, shape, dim)


def _head_block_ones():
    r = _iota((MXU_DIM, MXU_DIM), 0) // HEAD
    c = _iota((MXU_DIM, MXU_DIM), 1) // HEAD
    return jnp.where(r == c, 1.0, 0.0).astype(BF16)


def _head_sum_bcast(x, ones_bd):
    d = x.shape[1]
    parts = [_dot(x[:, p:p + MXU_DIM], ones_bd) for p in range(0, d, MXU_DIM)]
    return jnp.concatenate(parts, axis=1)


def _layer_norm(x, g, b, eps):
    mu = jnp.mean(x, axis=-1, keepdims=True)
    xc = x - mu
    var = jnp.mean(xc * xc, axis=-1, keepdims=True)
    return xc * lax.rsqrt(var + eps) * g + b


def _rwkv_proj_kernel(x_ref, xp_ref, mu_ref, wr_ref, wk_ref, wv_ref, w0_ref, w1_ref, w2_ref,
                      a0_ref, a1_ref, a2_ref, g1_ref, g2_ref, kk_ref, ka_ref, rk_ref,
                      rt_ref, kt_ref, bt_ref, at_ref, v_ref, bonus_ref, g_ref, gl_ref):
    i = pl.program_id(1)
    x = x_ref[...]
    tm, d = x.shape
    prev_row = jnp.where(i == 0, 0.0, xp_ref[7:8, :])
    row = _iota((tm, d), 0)
    xprev = jnp.where(row == 0, prev_row, pltpu.roll(x, 1, 0))
    xx = xprev - x
    mu = mu_ref[...]

    def mix(j):
        return (x + xx * mu[j:j + 1, :]).astype(BF16)

    r = _dot(mix(0), wr_ref[...])
    wraw = w0_ref[...] + _dot(jnp.tanh(_dot(mix(1), w1_ref[...])), w2_ref[...])
    k = _dot(mix(2), wk_ref[...])
    v = _dot(mix(3), wv_ref[...])
    a = jax.nn.sigmoid(a0_ref[...] + _dot(_dot(mix(4), a1_ref[...]), a2_ref[...]))
    g = _dot(jax.nn.sigmoid(_dot(mix(5), g1_ref[...])), g2_ref[...])

    w = jnp.minimum(wraw, 0.0) - jnp.log1p(jnp.exp(-jnp.abs(wraw))) - 0.5
    ld = -jnp.exp(w)

    ones_bd = _head_block_ones()
    kk = k * kk_ref[...]
    kk = kk / jnp.maximum(jnp.sqrt(_head_sum_bcast(kk * kk, ones_bd)), 1e-12)
    k = k * (1.0 + (a - 1.0) * ka_ref[...])
    bonus = _head_sum_bcast(r * k * rk_ref[...], ones_bd) * v

    tr = _iota((MXU_DIM, MXU_DIM), 0)
    tc = _iota((MXU_DIM, MXU_DIM), 1)
    tri = jnp.where((tr // WKV_L == tc // WKV_L) & (tc <= tr), 1.0, 0.0).astype(BF16)
    ld_hi = ld.astype(BF16)
    ld_lo = (ld - ld_hi.astype(F32)).astype(BF16)
    cs = jnp.concatenate(
        [jnp.dot(tri, ld_hi[p:p + MXU_DIM, :], preferred_element_type=F32)
         + jnp.dot(tri, ld_lo[p:p + MXU_DIM, :], preferred_element_type=F32)
         for p in range(0, tm, MXU_DIM)], axis=0)

    gam = jnp.exp(cs)
    igam = jnp.exp(-cs)
    rt_ref[...] = (r * gam).astype(BF16)
    kt_ref[...] = (k * igam).astype(BF16)
    bt_ref[...] = (kk * a * igam).astype(BF16)
    at_ref[...] = (-kk * jnp.exp(cs - ld)).astype(BF16)
    v_ref[...] = v.astype(BF16)
    bonus_ref[...] = bonus.astype(BF16)
    g_ref[...] = g.astype(BF16)
    gl_ref[...] = jnp.exp(cs.reshape(tm // WKV_L, WKV_L, d)[:, WKV_L - 1, :])


def _rwkv_proj(x, p, tm=512):
    bsz, t, d = x.shape
    nt = t // tm
    row_spec = pl.BlockSpec((None, tm, d), lambda b, i: (b, i, 0))
    prev_spec = pl.BlockSpec((None, 8, d), lambda b, i: (b, jnp.maximum(i * (tm // 8) - 1, 0), 0))

    def full(arr):
        nd = arr.ndim
        return pl.BlockSpec(arr.shape, lambda b, i: (0,) * nd, pipeline_mode=pl.Buffered(1))

    weights = [p["mu"], p["w_r"], p["w_k"], p["w_v"], p["w0"], p["w1"], p["w2"], p["a0"], p["a1"], p["a2"],
               p["g1"], p["g2"], p["k_k"], p["k_a"], p["r_k"]]
    out_bf = jax.ShapeDtypeStruct((bsz, t, d), BF16)
    out_shape = [out_bf] * 7 + [jax.ShapeDtypeStruct((bsz, t // WKV_L, d), F32)]
    out_specs = [row_spec] * 7 + [pl.BlockSpec((None, tm // WKV_L, d), lambda b, i: (b, i, 0))]
    return pl.pallas_call(
        _rwkv_proj_kernel,
        grid=(bsz, nt),
        in_specs=[row_spec, prev_spec] + [full(w) for w in weights],
        out_specs=out_specs,
        out_shape=out_shape,
        compiler_params=pltpu.CompilerParams(dimension_semantics=("parallel", "parallel"),
                                             vmem_limit_bytes=VMEM_LIMIT),
        name="rwkv_proj",
    )(x, x, *weights)


def _wkv_kernel(rt_ref, kt_ref, bt_ref, at_ref, v_ref, gl_ref, y_ref, st_ref):
    tb, d = rt_ref.shape
    L, W = WKV_L, MXU_DIM
    npack = d // W
    row = _iota((L, W), 0)
    sidx = _iota((L, W), 1) % L
    strict = sidx < row
    incl = sidx <= row
    eye = sidx == row
    r2 = _iota((W, W), 0)
    c2 = _iota((W, W), 1)
    bd = (r2 // L) == (c2 // L)
    diag = r2 == c2

    def stack(xb):
        return jnp.where(bd, jnp.concatenate([xb] * PACK, axis=0), jnp.zeros((), xb.dtype))

    @pl.when(pl.program_id(1) == 0)
    def _():
        st_ref[...] = jnp.zeros_like(st_ref)

    def chunk(c, pk):
        sl = pl.ds(pl.multiple_of(c * L, L), L)
        ln = slice(pk * W, (pk + 1) * W)
        rt = rt_ref[sl, ln]
        kt = kt_ref[sl, ln]
        bt = bt_ref[sl, ln]
        at = at_ref[sl, ln]
        v = v_ref[sl, ln]
        gl = gl_ref[pl.ds(c, 1), ln]

        gmat = _dot_nt(jnp.concatenate([at, rt], axis=0),
                       jnp.concatenate([stack(bt), stack(kt)], axis=0))
        yield
        ab = gmat[:L, :W]
        ak = gmat[:L, W:]
        rb = gmat[L:, :W]
        rk = gmat[L:, W:]

        n1 = jnp.where(strict, ab, 0.0)
        tm_ = jnp.where(eye, 1.0, 0.0) + n1
        cur = n1.astype(BF16)
        cur = _dot(cur, stack(cur)).astype(BF16)
        sv = stack(v)
        res = _dot(jnp.concatenate([jnp.where(strict, ak, 0.0), jnp.where(incl, rk, 0.0)], axis=0), sv)
        x1 = res[:L]
        rkv = res[L:]
        yield
        for _ in range(4):
            res = _dot(jnp.concatenate([cur, tm_.astype(BF16)], axis=0), stack(cur))
            tm_ = tm_ + res[L:]
            cur = res[:L].astype(BF16)
            yield
        tm_ = tm_ + _dot(tm_, stack(cur))
        yield

        uw = _dot(tm_, jnp.concatenate([stack(x1.astype(BF16)), stack(at)], axis=1))
        yield
        u0 = uw[:, :W].astype(BF16)
        wm = uw[:, W:].astype(BF16)
        yr = _dot(jnp.where(incl, rb, 0.0), jnp.concatenate([stack(u0), stack(wm)], axis=1))
        bh = (bt.astype(F32) * gl).astype(BF16)
        kh = (kt.astype(F32) * gl).astype(BF16)
        rhs2 = jnp.concatenate([jnp.concatenate([u0, wm], axis=1),
                                jnp.concatenate([v, jnp.zeros((L, W), BF16)], axis=1)], axis=0)
        pq = _dot_tn(jnp.concatenate([bh, kh], axis=0), rhs2)
        yield
        y0 = yr[:, :W] + rkv
        rp = rt.astype(F32) + yr[:, W:]
        q = jnp.where(bd, pq[:, :W], 0.0)
        p = jnp.where(bd, pq[:, W:], 0.0) + jnp.where(diag, gl, 0.0)

        res = _dot(jnp.concatenate([p.astype(BF16), rp.astype(BF16)], axis=0), st_ref[pk])
        yield
        st_ref[pk] = res[:W] + q
        y_ref[sl, ln] = (y0 + res[W:]).astype(y_ref.dtype)

    def body(c, carry):
        for _ in itertools.zip_longest(*[chunk(c, pk) for pk in range(npack)]):
            pass
        return carry

    lax.fori_loop(0, tb // L, body, 0)


def _wkv(rt, kt, bt, at, v, gl, tb=512):
    bsz, t, d = rt.shape
    seq_spec = pl.BlockSpec((None, tb, d), lambda b, i: (b, i, 0))
    gl_spec = pl.BlockSpec((None, tb // WKV_L, d), lambda b, i: (b, i, 0))
    return pl.pallas_call(
        _wkv_kernel,
        grid=(bsz, t // tb),
        in_specs=[seq_spec] * 5 + [gl_spec],
        out_specs=seq_spec,
        out_shape=jax.ShapeDtypeStruct((bsz, t, d), BF16),
        scratch_shapes=[pltpu.VMEM((d // MXU_DIM, MXU_DIM, MXU_DIM), F32)],
        compiler_params=pltpu.CompilerParams(dimension_semantics=("parallel", "arbitrary"),
                                             vmem_limit_bytes=VMEM_LIMIT),
        name="wkv7",
    )(rt, kt, bt, at, v, gl)


def _post_block_kernel(*refs, rwkv, project):
    it = iter(refs)
    x_ref = next(it)
    y_ref = next(it)
    if rwkv:
        bonus_ref, g_ref, lnxg_ref, lnxb_ref = next(it), next(it), next(it), next(it)
    wo_ref, ln_ref, w1_ref, w2_ref = next(it), next(it), next(it), next(it)
    if project:
        wq_ref, wkv_ref = next(it), next(it)
    out_ref = next(it)
    if project:
        q_ref, k_ref, v_ref = next(it), next(it), next(it)

    x = x_ref[...]
    d = x.shape[1]
    if rwkv:
        ones_bd = _head_block_ones()
        y = y_ref[...].astype(F32)
        mean = _head_sum_bcast(y, ones_bd) * (1.0 / HEAD)
        yc = y - mean
        var = _head_sum_bcast(yc * yc, ones_bd) * (1.0 / HEAD)
        yn = yc * lax.rsqrt(var + GN_EPS) * lnxg_ref[...] + lnxb_ref[...]
        mixed = ((yn + bonus_ref[...].astype(F32)) * g_ref[...].astype(F32)).astype(BF16)
    else:
        mixed = y_ref[...]
    ln = ln_ref[...]
    x1 = _layer_norm(ALPHA * x + _dot(mixed, wo_ref[...]), ln[0:1, :], ln[1:2, :], LN_EPS)
    hid = jnp.maximum(_dot(x1, w1_ref[...]), 0.0)
    x2 = _layer_norm(ALPHA * x1 + _dot(hid * hid, w2_ref[...]), ln[2:3, :], ln[3:4, :], LN_EPS)
    out_ref[...] = x2
    if project:
        x2b = x2.astype(BF16)
        q_ref[...] = (_dot(x2b, wq_ref[...]) * (HEAD ** -0.5)).astype(BF16)
        kv = _dot(x2b, wkv_ref[...])
        k_ref[...] = kv[:, :d].astype(BF16)
        v_ref[...] = kv[:, d:].astype(BF16)


def _post_block(x, y, rw, w_o, ln, w1, w2, proj, tm=256):
    bsz, t, d = x.shape
    rwkv = rw is not None
    project = proj is not None
    row_spec = pl.BlockSpec((None, tm, d), lambda b, i: (b, i, 0))

    def full(arr):
        nd = arr.ndim
        return pl.BlockSpec(arr.shape, lambda b, i: (0,) * nd, pipeline_mode=pl.Buffered(1))

    args = [x, y]
    specs = [row_spec, row_spec]
    if rwkv:
        args += [rw["bonus"], rw["g"], rw["lnx_g"], rw["lnx_b"]]
        specs += [row_spec, row_spec, full(rw["lnx_g"]), full(rw["lnx_b"])]
    args += [w_o, ln, w1, w2]
    specs += [full(w_o), full(ln), full(w1), full(w2)]
    out_shape = [jax.ShapeDtypeStruct((bsz, t, d), F32)]
    out_specs = [row_spec]
    if project:
        args += [proj["w_q"], proj["w_kv"]]
        specs += [full(proj["w_q"]), full(proj["w_kv"])]
        out_shape += [jax.ShapeDtypeStruct((bsz, t, d), BF16)] * 3
        out_specs += [row_spec] * 3
    return pl.pallas_call(
        functools.partial(_post_block_kernel, rwkv=rwkv, project=project),
        grid=(bsz, t // tm),
        in_specs=specs,
        out_specs=out_specs,
        out_shape=out_shape,
        compiler_params=pltpu.CompilerParams(dimension_semantics=("parallel", "parallel"),
                                             vmem_limit_bytes=VMEM_LIMIT),
        name="post_rwkv" if rwkv else "post_attn",
    )(*args)


def _t5_bucket_np(rel):
    nb = REL_BUCKETS // 2
    max_exact = nb // 2
    n = np.abs(rel)
    thresholds = [int(math.ceil(max_exact * (REL_MAX_DIST / max_exact) ** (m / (nb - max_exact)) - 1e-9))
                  for m in range(1, nb - max_exact)]
    large = max_exact + sum((n >= th).astype(np.int64) for th in thresholds)
    return np.where(rel > 0, nb, 0) + np.where(n < max_exact, n, np.minimum(large, nb - 1))


def _bias_bucket_tables():
    ql = np.arange(Q_BLOCK)[:, None]
    kl = np.arange(KEY_TILE)[None, :]
    out = np.zeros((4, Q_BLOCK, KEY_TILE), np.int32)
    for par in range(2):
        qpos = Q_BLOCK * par + ql
        out[2 * par + 0] = _t5_bucket_np(kl - KEY_TILE - qpos)
        allowed = (kl // ATT_CHUNK) <= (qpos // ATT_CHUNK)
        out[2 * par + 1] = np.where(allowed, _t5_bucket_np(kl - qpos), -1)
    return out


def _bias_table_kernel(bucket_ref, relb_ref, out_ref):
    h = pl.program_id(0)
    bucket = bucket_ref[...]
    acc = jnp.where(bucket < 0, NEG_BIG, 0.0).astype(F32)
    for b in range(REL_BUCKETS):
        acc = jnp.where(bucket == b, relb_ref[b, h], acc)
    out_ref[...] = acc


def _bias_tables(rel_bias):
    nh = rel_bias.shape[1]
    buckets = jnp.asarray(_bias_bucket_tables())
    return pl.pallas_call(
        _bias_table_kernel,
        grid=(nh,),
        in_specs=[pl.BlockSpec(buckets.shape, lambda h: (0, 0, 0)),
                  pl.BlockSpec(memory_space=pltpu.SMEM)],
        out_specs=pl.BlockSpec((None,) + buckets.shape, lambda h: (h, 0, 0, 0)),
        out_shape=jax.ShapeDtypeStruct((nh,) + buckets.shape, F32),
        name="t5_bias_tables",
    )(buckets, rel_bias)


def _diff_attn_kernel(q_ref, k_ref, v_ref, tab_ref, relb_ref, lam_ref, sg_ref, o_ref, s_scr, p_scr, *,
                      lambda_init):
    h = pl.program_id(1)
    t = q_ref.shape[0]
    qb, kt_, dh = Q_BLOCK, KEY_TILE, HEAD
    c_far = relb_ref[REL_BUCKETS // 2 - 1, h]
    lam = lam_ref[...]
    lam_full = (jnp.exp(jnp.sum(lam[0:1, :] * lam[1:2, :], axis=-1, keepdims=True))
                - jnp.exp(jnp.sum(lam[2:3, :] * lam[3:4, :], axis=-1, keepdims=True)) + lambda_init)
    qp = 2 * qb
    lane = _iota((qp, 2 * dh), 1)
    sg = sg_ref[...] * (1.0 - lambda_init)
    prev_tab = jnp.concatenate([tab_ref[0], tab_ref[2]] * 2, axis=0)
    last_tab = jnp.concatenate([tab_ref[1], tab_ref[3]] * 2, axis=0)

    for jl in range(t // kt_):
        qs = slice(jl * qp, (jl + 1) * qp)
        q = q_ref[qs, :]
        zero = jnp.zeros((), q.dtype)
        qm = jnp.concatenate([jnp.where(lane < dh, q, zero), jnp.where(lane >= dh, q, zero)], axis=0)
        mx = jnp.full((2 * qp, LANES), NEG_BIG, F32)
        for j in range(jl + 1):
            ks = slice(j * kt_, (j + 1) * kt_)
            s = _dot_nt(qm, k_ref[ks, :])
            if j == jl:
                s = s + last_tab
            elif j == jl - 1:
                s = s + prev_tab
            else:
                s = s + c_far
            s_scr[:, ks] = s
            mx = jnp.maximum(mx, jnp.maximum(s[:, :LANES], s[:, LANES:]))
        m = jnp.max(mx, axis=-1, keepdims=True)
        ls = jnp.zeros((2 * qp, LANES), F32)
        for j in range(jl + 1):
            ks = slice(j * kt_, (j + 1) * kt_)
            pexp = jnp.exp(s_scr[:, ks] - m)
            ls = ls + (pexp[:, :LANES] + pexp[:, LANES:])
            p_scr[:, ks] = pexp.astype(BF16)
        kend = (jl + 1) * kt_
        acc = jnp.dot(p_scr[:, :kend], v_ref[:kend, :], preferred_element_type=F32)
        inv_l = 1.0 / jnp.sum(ls, axis=-1, keepdims=True)
        o = acc[:qp] * inv_l[:qp] - lam_full * (acc[qp:] * inv_l[qp:])
        o = o * lax.rsqrt(jnp.mean(o * o, axis=-1, keepdims=True) + SUBLN_EPS)
        o_ref[qs, :] = (o * sg).astype(o_ref.dtype)


def _diff_attn(q, k, v, tabs, rel_bias, lam, subln_g, lambda_init):
    bsz, t, d = q.shape
    nh = d // (2 * HEAD)
    seq_spec = pl.BlockSpec((None, t, 2 * HEAD), lambda b, h: (b, 0, h))
    return pl.pallas_call(
        functools.partial(_diff_attn_kernel, lambda_init=lambda_init),
        grid=(bsz, nh),
        in_specs=[seq_spec, seq_spec, seq_spec,
                  pl.BlockSpec((None,) + tabs.shape[1:], lambda b, h: (h, 0, 0, 0)),
                  pl.BlockSpec(memory_space=pltpu.SMEM),
                  pl.BlockSpec(lam.shape, lambda b, h: (0, 0)),
                  pl.BlockSpec(subln_g.shape, lambda b, h: (0, 0))],
        out_specs=seq_spec,
        out_shape=jax.ShapeDtypeStruct((bsz, t, d), BF16),
        scratch_shapes=[pltpu.VMEM((4 * Q_BLOCK, t), F32),
                        pltpu.VMEM((4 * Q_BLOCK, t), BF16)],
        compiler_params=pltpu.CompilerParams(dimension_semantics=("parallel", "parallel"),
                                             vmem_limit_bytes=VMEM_LIMIT),
        name="diff_attn",
    )(q, k, v, tabs, rel_bias, lam, subln_g)


def kernel(x, a_mu, a_w_r, a_w_k, a_w_v, a_w_o, a_w0, a_w1, a_w2, a_a0, a_a1, a_a2, a_g1, a_g2, a_k_k, a_k_a,
           a_r_k, a_lnx_g, a_lnx_b, b_w_kv, b_w_q, b_lam, b_subln_g, b_w_o, rel_bias, mlp_w1, mlp_w2, ln_g, ln_b):
    bsz, t, d = x.shape
    assert d % MXU_DIM == 0 and t % 512 == 0
    bf = lambda w: w.astype(BF16)
    row = lambda w: w.reshape(1, d)

    p = dict(mu=a_mu[0], w_r=bf(a_w_r[0]), w_k=bf(a_w_k[0]), w_v=bf(a_w_v[0]), w0=row(a_w0[0]),
             w1=bf(a_w1[0]), w2=bf(a_w2[0]), a0=row(a_a0[0]), a1=bf(a_a1[0]), a2=bf(a_a2[0]),
             g1=bf(a_g1[0]), g2=bf(a_g2[0]), k_k=row(a_k_k[0]), k_a=row(a_k_a[0]), r_k=row(a_r_k[0]))
    rt, kt, bt, at, v, bonus, g, gl = _rwkv_proj(x, p)
    y = _wkv(rt, kt, bt, at, v, gl)
    ln0 = jnp.stack([ln_g[0, 0], ln_b[0, 0], ln_g[0, 1], ln_b[0, 1]])
    x, q, k_sh, v_sh = _post_block(
        x, y, dict(bonus=bonus, g=g, lnx_g=row(a_lnx_g[0]), lnx_b=row(a_lnx_b[0])),
        bf(a_w_o[0]), ln0, bf(mlp_w1[0]), bf(mlp_w2[0]), dict(w_q=bf(b_w_q[0]), w_kv=bf(b_w_kv)))

    lambda_init = 0.8 - 0.6 * math.exp(-0.3 * 1)
    tabs = _bias_tables(rel_bias)
    o = _diff_attn(q, k_sh, v_sh, tabs, rel_bias, b_lam[0], b_subln_g[0].reshape(1, 2 * HEAD), lambda_init)
    ln1 = jnp.stack([ln_g[1, 0], ln_b[1, 0], ln_g[1, 1], ln_b[1, 1]])
    (x,) = _post_block(x, o, None, bf(b_w_o[0]), ln1, bf(mlp_w1[1]), bf(mlp_w2[1]), None)
    return x
```

```python
import functools
import itertools
import math

import numpy as np
import jax
import jax.numpy as jnp
from jax import lax
from jax.experimental import pallas as pl
from jax.experimental.pallas import tpu as pltpu

F32 = jnp.float32
BF16 = jnp.bfloat16

DEPTH = 2
ALPHA = (2.0 * DEPTH) ** 0.25
HEAD = 64
GN_EPS = 64e-5
SUBLN_EPS = 1e-5
LN_EPS = 1e-5
REL_BUCKETS = 32
REL_MAX_DIST = 128
ATT_CHUNK = 64
Q_BLOCK = 128

LANES = 128
MXU_DIM = 256
WKV_L = 64
PACK = MXU_DIM // HEAD
KEY_TILE = 256
NEG_BIG = -1e30
LOG2E = math.log2(math.e)
VMEM_LIMIT = 56 * 1024 * 1024


def _dot(a, b):
    return jnp.dot(a.astype(BF16), b.astype(BF16), preferred_element_type=F32)


def _dot_nt(a, b):
    return lax.dot_general(a.astype(BF16), b.astype(BF16), (((1,), (1,)), ((), ())),
                           preferred_element_type=F32)


def _dot_tn(a, b):
    return lax.dot_general(a.astype(BF16), b.astype(BF16), (((0,), (0,)), ((), ())),
                           preferred_element_type=F32)


def _iota(shape, dim):
    return lax.broadcasted_iota(jnp.int32, shape, dim)


def _head_block_ones():
    r = _iota((MXU_DIM, MXU_DIM), 0) // HEAD
    c = _iota((MXU_DIM, MXU_DIM), 1) // HEAD
    return jnp.where(r == c, 1.0, 0.0).astype(BF16)


def _head_sum_bcast(x, ones_bd):
    d = x.shape[1]
    parts = [_dot(x[:, p:p + MXU_DIM], ones_bd) for p in range(0, d, MXU_DIM)]
    return jnp.concatenate(parts, axis=1)


def _layer_norm(x, g, b, eps):
    mu = jnp.mean(x, axis=-1, keepdims=True)
    xc = x - mu
    var = jnp.mean(xc * xc, axis=-1, keepdims=True)
    return xc * lax.rsqrt(var + eps) * g + b


def _rwkv_proj_kernel(x_ref, xp_ref, mu_ref, wr_ref, wk_ref, wv_ref, w0_ref, w1_ref, w2_ref,
                      a0_ref, a1_ref, a2_ref, g1_ref, g2_ref, kk_ref, ka_ref, rk_ref,
                      rt_ref, kt_ref, bt_ref, at_ref, v_ref, bonus_ref, g_ref, gl_ref):
    i = pl.program_id(1)
    x = x_ref[...]
    tm, d = x.shape
    prev_row = jnp.where(i == 0, 0.0, xp_ref[7:8, :])
    row = _iota((tm, d), 0)
    xprev = jnp.where(row == 0, prev_row, pltpu.roll(x, 1, 0))
    xx = xprev - x
    mu = mu_ref[...]

    def mix(j):
        return (x + xx * mu[j:j + 1, :]).astype(BF16)

    r = _dot(mix(0), wr_ref[...])
    wraw = w0_ref[...] + _dot(jnp.tanh(_dot(mix(1), w1_ref[...])), w2_ref[...])
    k = _dot(mix(2), wk_ref[...])
    v = _dot(mix(3), wv_ref[...])
    a = jax.nn.sigmoid(a0_ref[...] + _dot(_dot(mix(4), a1_ref[...]), a2_ref[...]))
    g = _dot(jax.nn.sigmoid(_dot(mix(5), g1_ref[...])), g2_ref[...])

    w = jnp.minimum(wraw, 0.0) - jnp.log1p(jnp.exp(-jnp.abs(wraw))) - 0.5
    ld = -jnp.exp(w)

    ones_bd = _head_block_ones()
    kk = k * kk_ref[...]
    kk = kk / jnp.maximum(jnp.sqrt(_head_sum_bcast(kk * kk, ones_bd)), 1e-12)
    k = k * (1.0 + (a - 1.0) * ka_ref[...])
    bonus = _head_sum_bcast(r * k * rk_ref[...], ones_bd) * v

    tr = _iota((MXU_DIM, MXU_DIM), 0)
    tc = _iota((MXU_DIM, MXU_DIM), 1)
    tri = jnp.where((tr // WKV_L == tc // WKV_L) & (tc <= tr), 1.0, 0.0).astype(BF16)
    ld_hi = ld.astype(BF16)
    ld_lo = (ld - ld_hi.astype(F32)).astype(BF16)
    cs = jnp.concatenate(
        [jnp.dot(tri, ld_hi[p:p + MXU_DIM, :], preferred_element_type=F32)
         + jnp.dot(tri, ld_lo[p:p + MXU_DIM, :], preferred_element_type=F32)
         for p in range(0, tm, MXU_DIM)], axis=0)

    gam = jnp.exp(cs)
    igam = jnp.exp(-cs)
    rt_ref[...] = (r * gam).astype(BF16)
    kt_ref[...] = (k * igam).astype(BF16)
    bt_ref[...] = (kk * a * igam).astype(BF16)
    at_ref[...] = (-kk * jnp.exp(cs - ld)).astype(BF16)
    v_ref[...] = v.astype(BF16)
    bonus_ref[...] = bonus.astype(BF16)
    g_ref[...] = g.astype(BF16)
    gl_ref[...] = jnp.exp(cs.reshape(tm // WKV_L, WKV_L, d)[:, WKV_L - 1, :])


def _rwkv_proj(x, p, tm=512):
    bsz, t, d = x.shape
    nt = t // tm
    row_spec = pl.BlockSpec((None, tm, d), lambda b, i: (b, i, 0))
    prev_spec = pl.BlockSpec((None, 8, d), lambda b, i: (b, jnp.maximum(i * (tm // 8) - 1, 0), 0))

    def full(arr):
        nd = arr.ndim
        return pl.BlockSpec(arr.shape, lambda b, i: (0,) * nd, pipeline_mode=pl.Buffered(1))

    weights = [p["mu"], p["w_r"], p["w_k"], p["w_v"], p["w0"], p["w1"], p["w2"], p["a0"], p["a1"], p["a2"],
               p["g1"], p["g2"], p["k_k"], p["k_a"], p["r_k"]]
    out_bf = jax.ShapeDtypeStruct((bsz, t, d), BF16)
    out_shape = [out_bf] * 7 + [jax.ShapeDtypeStruct((bsz, t // WKV_L, d), F32)]
    out_specs = [row_spec] * 7 + [pl.BlockSpec((None, tm // WKV_L, d), lambda b, i: (b, i, 0))]
    return pl.pallas_call(
        _rwkv_proj_kernel,
        grid=(bsz, nt),
        in_specs=[row_spec, prev_spec] + [full(w) for w in weights],
        out_specs=out_specs,
        out_shape=out_shape,
        compiler_params=pltpu.CompilerParams(dimension_semantics=("parallel", "parallel"),
                                             vmem_limit_bytes=VMEM_LIMIT),
        name="rwkv_proj",
    )(x, x, *weights)


def _wkv_kernel(rt_ref, kt_ref, bt_ref, at_ref, v_ref, gl_ref, y_ref, st_ref, t_scr, rb_scr, x1_scr):
    tb, d = rt_ref.shape
    L, W = WKV_L, MXU_DIM
    npack = d // W
    row = _iota((L, W), 0)
    sidx = _iota((L, W), 1) % L
    strict = sidx < row
    incl = sidx <= row
    eye = sidx == row
    r2 = _iota((W, W), 0)
    c2 = _iota((W, W), 1)
    bd = (r2 // L) == (c2 // L)
    diag = r2 == c2

    def stack(xb):
        return jnp.where(bd, jnp.concatenate([xb] * PACK, axis=0), jnp.zeros((), xb.dtype))

    @pl.when(pl.program_id(1) == 0)
    def _():
        st_ref[...] = jnp.zeros_like(st_ref)

    def rows(c):
        return pl.ds(pl.multiple_of(c * L, L), L)

    def prepare(c, pk, slot):
        sl = rows(c)
        ln = slice(pk * W, (pk + 1) * W)
        rt = rt_ref[sl, ln]
        kt = kt_ref[sl, ln]
        bt = bt_ref[sl, ln]
        at = at_ref[sl, ln]
        v = v_ref[sl, ln]
        gmat = _dot_nt(jnp.concatenate([at, rt], axis=0),
                       jnp.concatenate([stack(bt), stack(kt)], axis=0))
        yield
        ab = gmat[:L, :W]
        ak = gmat[:L, W:]
        rb = gmat[L:, :W]
        rk = gmat[L:, W:]
        n1 = jnp.where(strict, ab, 0.0)
        tm_ = jnp.where(eye, 1.0, 0.0) + n1
        cur = n1.astype(BF16)
        cur = _dot(cur, stack(cur)).astype(BF16)
        res = _dot(jnp.concatenate([jnp.where(strict, ak, 0.0), jnp.where(incl, rk, 0.0)], axis=0), stack(v))
        yield
        for _ in range(4):
            nres = _dot(jnp.concatenate([cur, tm_.astype(BF16)], axis=0), stack(cur))
            tm_ = tm_ + nres[L:]
            cur = nres[:L].astype(BF16)
            yield
        tm_ = tm_ + _dot(tm_, stack(cur))
        yield
        t_scr[slot, pk] = tm_.astype(BF16)
        rb_scr[slot, pk] = jnp.where(incl, rb, 0.0).astype(BF16)
        x1_scr[slot, pk] = res

    def advance(c, pk, slot):
        sl = rows(c)
        ln = slice(pk * W, (pk + 1) * W)
        tm_ = t_scr[slot, pk]
        rbi = rb_scr[slot, pk]
        x1r = x1_scr[slot, pk]
        rt = rt_ref[sl, ln]
        kt = kt_ref[sl, ln]
        bt = bt_ref[sl, ln]
        at = at_ref[sl, ln]
        v = v_ref[sl, ln]
        gl = gl_ref[pl.ds(c, 1), ln]
        st = st_ref[pk]
        ra = _dot(jnp.concatenate([rt, at], axis=0), st)
        yield
        ub = _dot(tm_, stack((x1r[:L] + ra[L:]).astype(BF16))).astype(BF16)
        yield
        yr = _dot(rbi, stack(ub))
        bh = (bt.astype(F32) * gl).astype(BF16)
        kh = (kt.astype(F32) * gl).astype(BF16)
        upd = _dot_tn(jnp.concatenate([bh, kh], axis=0), jnp.concatenate([ub, v], axis=0))
        yield
        glcol = jnp.sum(jnp.where(diag, gl, 0.0), axis=1, keepdims=True)
        st_ref[pk] = glcol * st + jnp.where(bd, upd, 0.0)
        y_ref[sl, ln] = (ra[:L] + yr + x1r[L:]).astype(y_ref.dtype)

    def run(*gens):
        for _ in itertools.zip_longest(*gens):
            pass

    npair = tb // (2 * L)
    packs = range(npack)

    def advance2(c, s0, s1, pk):
        return itertools.chain(advance(c, pk, s0), advance(c + 1, pk, s1))

    run(*([prepare(0, pk, 0) for pk in packs] + [prepare(1, pk, 1) for pk in packs]))

    def body(i, carry):
        c = 2 * i
        s0 = c % 4
        run(*([advance2(c, s0, s0 + 1, pk) for pk in packs]
              + [prepare(c + 2, pk, (s0 + 2) % 4) for pk in packs]
              + [prepare(c + 3, pk, (s0 + 3) % 4) for pk in packs]))
        return carry

    lax.fori_loop(0, npair - 1, body, 0)
    c_last = 2 * (npair - 1)
    run(*[advance2(c_last, c_last % 4, c_last % 4 + 1, pk) for pk in packs])


def _wkv(rt, kt, bt, at, v, gl, tb=1024):
    bsz, t, d = rt.shape
    npack = d // MXU_DIM
    seq_spec = pl.BlockSpec((None, tb, d), lambda b, i: (b, i, 0))
    gl_spec = pl.BlockSpec((None, tb // WKV_L, d), lambda b, i: (b, i, 0))
    return pl.pallas_call(
        _wkv_kernel,
        grid=(bsz, t // tb),
        in_specs=[seq_spec] * 5 + [gl_spec],
        out_specs=seq_spec,
        out_shape=jax.ShapeDtypeStruct((bsz, t, d), BF16),
        scratch_shapes=[pltpu.VMEM((npack, MXU_DIM, MXU_DIM), F32),
                        pltpu.VMEM((4, npack, WKV_L, MXU_DIM), BF16),
                        pltpu.VMEM((4, npack, WKV_L, MXU_DIM), BF16),
                        pltpu.VMEM((4, npack, 2 * WKV_L, MXU_DIM), F32)],
        compiler_params=pltpu.CompilerParams(dimension_semantics=("parallel", "arbitrary"),
                                             vmem_limit_bytes=VMEM_LIMIT),
        name="wkv7",
    )(rt, kt, bt, at, v, gl)


def _post_block_kernel(*refs, rwkv, project):
    it = iter(refs)
    x_ref = next(it)
    y_ref = next(it)
    if rwkv:
        bonus_ref, g_ref, lnxg_ref, lnxb_ref = next(it), next(it), next(it), next(it)
    wo_ref, ln_ref, w1_ref, w2_ref = next(it), next(it), next(it), next(it)
    if project:
        wq_ref, wkv_ref = next(it), next(it)
    out_ref = next(it)
    if project:
        q_ref, k_ref, v_ref = next(it), next(it), next(it)

    x = x_ref[...]
    d = x.shape[1]
    if rwkv:
        ones_bd = _head_block_ones()
        y = y_ref[...].astype(F32)
        mean = _head_sum_bcast(y, ones_bd) * (1.0 / HEAD)
        yc = y - mean
        var = _head_sum_bcast(yc * yc, ones_bd) * (1.0 / HEAD)
        yn = yc * lax.rsqrt(var + GN_EPS) * lnxg_ref[...] + lnxb_ref[...]
        mixed = ((yn + bonus_ref[...].astype(F32)) * g_ref[...].astype(F32)).astype(BF16)
    else:
        mixed = y_ref[...]
    ln = ln_ref[...]
    x1 = _layer_norm(ALPHA * x + _dot(mixed, wo_ref[...]), ln[0:1, :], ln[1:2, :], LN_EPS)
    hid = jnp.maximum(_dot(x1, w1_ref[...]), 0.0)
    x2 = _layer_norm(ALPHA * x1 + _dot(hid * hid, w2_ref[...]), ln[2:3, :], ln[3:4, :], LN_EPS)
    out_ref[...] = x2
    if project:
        x2b = x2.astype(BF16)
        q_ref[...] = (_dot(x2b, wq_ref[...]) * (HEAD ** -0.5 * LOG2E)).astype(BF16)
        kv = _dot(x2b, wkv_ref[...])
        k_ref[...] = kv[:, :d].astype(BF16)
        v_ref[...] = kv[:, d:].astype(BF16)


def _post_block(x, y, rw, w_o, ln, w1, w2, proj, tm=256):
    bsz, t, d = x.shape
    rwkv = rw is not None
    project = proj is not None
    row_spec = pl.BlockSpec((None, tm, d), lambda b, i: (b, i, 0))

    def full(arr):
        nd = arr.ndim
        return pl.BlockSpec(arr.shape, lambda b, i: (0,) * nd, pipeline_mode=pl.Buffered(1))

    args = [x, y]
    specs = [row_spec, row_spec]
    if rwkv:
        args += [rw["bonus"], rw["g"], rw["lnx_g"], rw["lnx_b"]]
        specs += [row_spec, row_spec, full(rw["lnx_g"]), full(rw["lnx_b"])]
    args += [w_o, ln, w1, w2]
    specs += [full(w_o), full(ln), full(w1), full(w2)]
    out_shape = [jax.ShapeDtypeStruct((bsz, t, d), F32)]
    out_specs = [row_spec]
    if project:
        args += [proj["w_q"], proj["w_kv"]]
        specs += [full(proj["w_q"]), full(proj["w_kv"])]
        out_shape += [jax.ShapeDtypeStruct((bsz, t, d), BF16)] * 3
        out_specs += [row_spec] * 3
    return pl.pallas_call(
        functools.partial(_post_block_kernel, rwkv=rwkv, project=project),
        grid=(bsz, t // tm),
        in_specs=specs,
        out_specs=out_specs,
        out_shape=out_shape,
        compiler_params=pltpu.CompilerParams(dimension_semantics=("parallel", "parallel"),
                                             vmem_limit_bytes=VMEM_LIMIT),
        name="post_rwkv" if rwkv else "post_attn",
    )(*args)


def _t5_bucket_np(rel):
    nb = REL_BUCKETS // 2
    max_exact = nb // 2
    n = np.abs(rel)
    thresholds = [int(math.ceil(max_exact * (REL_MAX_DIST / max_exact) ** (m / (nb - max_exact)) - 1e-9))
                  for m in range(1, nb - max_exact)]
    large = max_exact + sum((n >= th).astype(np.int64) for th in thresholds)
    return np.where(rel > 0, nb, 0) + np.where(n < max_exact, n, np.minimum(large, nb - 1))


def _bias_bucket_tables():
    ql = np.arange(Q_BLOCK)[:, None]
    kl = np.arange(KEY_TILE)[None, :]
    out = np.zeros((4, Q_BLOCK, KEY_TILE), np.int32)
    for par in range(2):
        qpos = Q_BLOCK * par + ql
        out[2 * par + 0] = _t5_bucket_np(kl - KEY_TILE - qpos)
        allowed = (kl // ATT_CHUNK) <= (qpos // ATT_CHUNK)
        out[2 * par + 1] = np.where(allowed, _t5_bucket_np(kl - qpos), -1)
    return out


def _bias_table_kernel(bucket_ref, relb_ref, out_ref):
    h = pl.program_id(0)
    bucket = bucket_ref[...]
    acc = jnp.where(bucket < 0, NEG_BIG, 0.0).astype(F32)
    for b in range(REL_BUCKETS):
        acc = jnp.where(bucket == b, relb_ref[b, h], acc)
    out_ref[...] = jnp.where(bucket < 0, NEG_BIG, acc * LOG2E)


def _bias_tables(rel_bias):
    nh = rel_bias.shape[1]
    buckets = jnp.asarray(_bias_bucket_tables())
    return pl.pallas_call(
        _bias_table_kernel,
        grid=(nh,),
        in_specs=[pl.BlockSpec(buckets.shape, lambda h: (0, 0, 0)),
                  pl.BlockSpec(memory_space=pltpu.SMEM)],
        out_specs=pl.BlockSpec((None,) + buckets.shape, lambda h: (h, 0, 0, 0)),
        out_shape=jax.ShapeDtypeStruct((nh,) + buckets.shape, F32),
        name="t5_bias_tables",
    )(buckets, rel_bias)


def _diff_attn_kernel(q_ref, k_ref, v_ref, tab_ref, relb_ref, lam_ref, sg_ref, o_ref, s_scr, p_scr, ve_scr, *,
                      lambda_init):
    h = pl.program_id(1)
    t = q_ref.shape[0]
    qb, kt_, dh = Q_BLOCK, KEY_TILE, HEAD
    c_far = relb_ref[REL_BUCKETS // 2 - 1, h] * LOG2E
    lam = lam_ref[...]
    lam_full = (jnp.exp(jnp.sum(lam[0:1, :] * lam[1:2, :], axis=-1, keepdims=True))
                - jnp.exp(jnp.sum(lam[2:3, :] * lam[3:4, :], axis=-1, keepdims=True)) + lambda_init)
    qp = 2 * qb
    lane = _iota((qp, 2 * dh), 1)
    sg = sg_ref[...] * (1.0 - lambda_init)
    prev_tab = jnp.concatenate([tab_ref[0], tab_ref[2]] * 2, axis=0)
    last_tab = jnp.concatenate([tab_ref[1], tab_ref[3]] * 2, axis=0)

    ve_scr[:, :2 * dh] = v_ref[...]
    ve_scr[:, 2 * dh:] = jnp.ones((t, 2 * dh), BF16)

    npairs = t // kt_
    row_max = {}

    def logits(jl):
        slot = jl % 2
        q = q_ref[jl * qp:(jl + 1) * qp, :]
        zero = jnp.zeros((), q.dtype)
        qm = jnp.concatenate([jnp.where(lane < dh, q, zero), jnp.where(lane >= dh, q, zero)], axis=0)
        mx_far = jnp.full((2 * qp, LANES), NEG_BIG, F32)
        mx_near = jnp.full((2 * qp, LANES), NEG_BIG, F32)
        for j in range(jl + 1):
            ks = slice(j * kt_, (j + 1) * kt_)
            s = _dot_nt(qm, k_ref[ks, :])
            if j >= jl - 1:
                s = s + (last_tab if j == jl else prev_tab)
                mx_near = jnp.maximum(mx_near, jnp.maximum(s[:, :LANES], s[:, LANES:]))
            else:
                mx_far = jnp.maximum(mx_far, jnp.maximum(s[:, :LANES], s[:, LANES:]))
            s_scr[slot, :, ks] = s
            yield
        row_max[jl] = jnp.maximum(jnp.max(mx_near, axis=-1, keepdims=True),
                                  jnp.max(mx_far, axis=-1, keepdims=True) + c_far)

    def outputs(jl):
        slot = jl % 2
        m = row_max.pop(jl)
        m_far = m - c_far
        for j in range(jl + 1):
            ks = slice(j * kt_, (j + 1) * kt_)
            p_scr[:, ks] = jnp.exp2(s_scr[slot, :, ks] - (m if j >= jl - 1 else m_far)).astype(BF16)
            yield
        kend = (jl + 1) * kt_
        acc = jnp.dot(p_scr[:, :kend], ve_scr[:kend, :], preferred_element_type=F32)
        pv = acc[:, :2 * dh] / acc[:, 2 * dh:]
        o = pv[:qp] - lam_full * pv[qp:]
        o = o * lax.rsqrt(jnp.mean(o * o, axis=-1, keepdims=True) + SUBLN_EPS)
        o_ref[jl * qp:(jl + 1) * qp, :] = (o * sg).astype(o_ref.dtype)

    def run(*gens):
        for _ in itertools.zip_longest(*gens):
            pass

    run(logits(npairs - 1))
    for jl in reversed(range(npairs)):
        run(*([outputs(jl)] + ([logits(jl - 1)] if jl > 0 else [])))


def _diff_attn(q, k, v, tabs, rel_bias, lam, subln_g, lambda_init):
    bsz, t, d = q.shape
    nh = d // (2 * HEAD)
    seq_spec = pl.BlockSpec((None, t, 2 * HEAD), lambda b, h: (b, 0, h))
    return pl.pallas_call(
        functools.partial(_diff_attn_kernel, lambda_init=lambda_init),
        grid=(bsz, nh),
        in_specs=[seq_spec, seq_spec, seq_spec,
                  pl.BlockSpec((None,) + tabs.shape[1:], lambda b, h: (h, 0, 0, 0)),
                  pl.BlockSpec(memory_space=pltpu.SMEM),
                  pl.BlockSpec(lam.shape, lambda b, h: (0, 0)),
                  pl.BlockSpec(subln_g.shape, lambda b, h: (0, 0))],
        out_specs=seq_spec,
        out_shape=jax.ShapeDtypeStruct((bsz, t, d), BF16),
        scratch_shapes=[pltpu.VMEM((2, 4 * Q_BLOCK, t), F32),
                        pltpu.VMEM((4 * Q_BLOCK, t), BF16),
                        pltpu.VMEM((t, 4 * HEAD), BF16)],
        compiler_params=pltpu.CompilerParams(dimension_semantics=("parallel", "parallel"),
                                             vmem_limit_bytes=VMEM_LIMIT),
        name="diff_attn",
    )(q, k, v, tabs, rel_bias, lam, subln_g)


def kernel(x, a_mu, a_w_r, a_w_k, a_w_v, a_w_o, a_w0, a_w1, a_w2, a_a0, a_a1, a_a2, a_g1, a_g2, a_k_k, a_k_a,
           a_r_k, a_lnx_g, a_lnx_b, b_w_kv, b_w_q, b_lam, b_subln_g, b_w_o, rel_bias, mlp_w1, mlp_w2, ln_g, ln_b):
    bsz, t, d = x.shape
    assert d % MXU_DIM == 0 and t % 512 == 0
    bf = lambda w: w.astype(BF16)
    row = lambda w: w.reshape(1, d)

    p = dict(mu=a_mu[0], w_r=bf(a_w_r[0]), w_k=bf(a_w_k[0]), w_v=bf(a_w_v[0]), w0=row(a_w0[0]),
             w1=bf(a_w1[0]), w2=bf(a_w2[0]), a0=row(a_a0[0]), a1=bf(a_a1[0]), a2=bf(a_a2[0]),
             g1=bf(a_g1[0]), g2=bf(a_g2[0]), k_k=row(a_k_k[0]), k_a=row(a_k_a[0]), r_k=row(a_r_k[0]))
    rt, kt, bt, at, v, bonus, g, gl = _rwkv_proj(x, p)
    y = _wkv(rt, kt, bt, at, v, gl)
    ln0 = jnp.stack([ln_g[0, 0], ln_b[0, 0], ln_g[0, 1], ln_b[0, 1]])
    x, q, k_sh, v_sh = _post_block(
        x, y, dict(bonus=bonus, g=g, lnx_g=row(a_lnx_g[0]), lnx_b=row(a_lnx_b[0])),
        bf(a_w_o[0]), ln0, bf(mlp_w1[0]), bf(mlp_w2[0]), dict(w_q=bf(b_w_q[0]), w_kv=bf(b_w_kv)))

    lambda_init = 0.8 - 0.6 * math.exp(-0.3 * 1)
    tabs = _bias_tables(rel_bias)
    o = _diff_attn(q, k_sh, v_sh, tabs, rel_bias, b_lam[0], b_subln_g[0].reshape(1, 2 * HEAD), lambda_init)
    ln1 = jnp.stack([ln_g[1, 0], ln_b[1, 0], ln_g[1, 1], ln_b[1, 1]])
    (x,) = _post_block(x, o, None, bf(b_w_o[0]), ln1, bf(mlp_w1[1]), bf(mlp_w2[1]), None)
    return x
```

```python
import functools
import itertools
import math

import numpy as np
import jax
import jax.numpy as jnp
from jax import lax
from jax.experimental import pallas as pl
from jax.experimental.pallas import tpu as pltpu

F32 = jnp.float32
BF16 = jnp.bfloat16

DEPTH = 2
ALPHA = (2.0 * DEPTH) ** 0.25
HEAD = 64
GN_EPS = 64e-5
SUBLN_EPS = 1e-5
LN_EPS = 1e-5
REL_BUCKETS = 32
REL_MAX_DIST = 128
ATT_CHUNK = 64
Q_BLOCK = 128

LANES = 128
MXU_DIM = 256
WKV_L = 64
PACK = MXU_DIM // HEAD
KEY_TILE = 256
NEG_BIG = -1e30
LOG2E = math.log2(math.e)
VMEM_LIMIT = 56 * 1024 * 1024


def _dot(a, b):
    return jnp.dot(a.astype(BF16), b.astype(BF16), preferred_element_type=F32)


def _dot_nt(a, b):
    return lax.dot_general(a.astype(BF16), b.astype(BF16), (((1,), (1,)), ((), ())),
                           preferred_element_type=F32)


def _dot_tn(a, b):
    return lax.dot_general(a.astype(BF16), b.astype(BF16), (((0,), (0,)), ((), ())),
                           preferred_element_type=F32)


def _iota(shape, dim):
    return lax.broadcasted_iota(jnp.int32, shape, dim)


def _head_block_ones():
    r = _iota((MXU_DIM, MXU_DIM), 0) // HEAD
    c = _iota((MXU_DIM, MXU_DIM), 1) // HEAD
    return jnp.where(r == c, 1.0, 0.0).astype(BF16)


def _head_sum_bcast(x, ones_bd):
    d = x.shape[1]
    parts = [_dot(x[:, p:p + MXU_DIM], ones_bd) for p in range(0, d, MXU_DIM)]
    return jnp.concatenate(parts, axis=1)


def _run_staggered(gens):
    gens = list(gens)
    live = []
    while gens or live:
        if gens:
            live.append(gens.pop(0))
        for g in list(live):
            if next(g, StopIteration) is StopIteration:
                live.remove(g)


def _layer_norm(x, g, b, eps):
    mu = jnp.mean(x, axis=-1, keepdims=True)
    xc = x - mu
    var = jnp.mean(xc * xc, axis=-1, keepdims=True)
    return xc * lax.rsqrt(var + eps) * g + b


def _rwkv_proj_kernel(x_ref, xp_ref, mu_ref, wr_ref, wk_ref, wv_ref, w0_ref, w1_ref, w2_ref,
                      a0_ref, a1_ref, a2_ref, g1_ref, g2_ref, kk_ref, ka_ref, rk_ref,
                      rt_ref, kt_ref, bt_ref, at_ref, v_ref, bonus_ref, g_ref, gl_ref):
    i = pl.program_id(1)
    x = x_ref[...]
    tm, d = x.shape
    prev_row = jnp.where(i == 0, 0.0, xp_ref[7:8, :])
    row = _iota((tm, d), 0)
    xprev = jnp.where(row == 0, prev_row, pltpu.roll(x, 1, 0))
    xx = xprev - x
    mu = mu_ref[...]

    def mix(j):
        return (x + xx * mu[j:j + 1, :]).astype(BF16)

    r = _dot(mix(0), wr_ref[...])
    wraw = w0_ref[...] + _dot(jnp.tanh(_dot(mix(1), w1_ref[...])), w2_ref[...])
    k = _dot(mix(2), wk_ref[...])
    v = _dot(mix(3), wv_ref[...])
    a = jax.nn.sigmoid(a0_ref[...] + _dot(_dot(mix(4), a1_ref[...]), a2_ref[...]))
    g = _dot(jax.nn.sigmoid(_dot(mix(5), g1_ref[...])), g2_ref[...])

    w = jnp.minimum(wraw, 0.0) - jnp.log1p(jnp.exp(-jnp.abs(wraw))) - 0.5
    ld = -jnp.exp(w)

    ones_bd = _head_block_ones()
    kk = k * kk_ref[...]
    kk = kk / jnp.maximum(jnp.sqrt(_head_sum_bcast(kk * kk, ones_bd)), 1e-12)
    k = k * (1.0 + (a - 1.0) * ka_ref[...])
    bonus = _head_sum_bcast(r * k * rk_ref[...], ones_bd) * v

    tr = _iota((MXU_DIM, MXU_DIM), 0)
    tc = _iota((MXU_DIM, MXU_DIM), 1)
    tri = jnp.where((tr // WKV_L == tc // WKV_L) & (tc <= tr), 1.0, 0.0).astype(BF16)
    ld_hi = ld.astype(BF16)
    ld_lo = (ld - ld_hi.astype(F32)).astype(BF16)
    cs = jnp.concatenate(
        [jnp.dot(tri, ld_hi[p:p + MXU_DIM, :], preferred_element_type=F32)
         + jnp.dot(tri, ld_lo[p:p + MXU_DIM, :], preferred_element_type=F32)
         for p in range(0, tm, MXU_DIM)], axis=0)

    gam = jnp.exp(cs)
    igam = jnp.exp(-cs)
    rt_ref[...] = (r * gam).astype(BF16)
    kt_ref[...] = (k * igam).astype(BF16)
    bt_ref[...] = (kk * a * igam).astype(BF16)
    at_ref[...] = (-kk * jnp.exp(cs - ld)).astype(BF16)
    v_ref[...] = v.astype(BF16)
    bonus_ref[...] = bonus.astype(BF16)
    g_ref[...] = g.astype(BF16)
    gl_ref[...] = jnp.exp(cs.reshape(tm // WKV_L, WKV_L, d)[:, WKV_L - 1, :])


def _rwkv_proj(x, p, tm=512):
    bsz, t, d = x.shape
    nt = t // tm
    row_spec = pl.BlockSpec((None, tm, d), lambda b, i: (b, i, 0))
    prev_spec = pl.BlockSpec((None, 8, d), lambda b, i: (b, jnp.maximum(i * (tm // 8) - 1, 0), 0))

    def full(arr):
        nd = arr.ndim
        return pl.BlockSpec(arr.shape, lambda b, i: (0,) * nd, pipeline_mode=pl.Buffered(1))

    weights = [p["mu"], p["w_r"], p["w_k"], p["w_v"], p["w0"], p["w1"], p["w2"], p["a0"], p["a1"], p["a2"],
               p["g1"], p["g2"], p["k_k"], p["k_a"], p["r_k"]]
    out_bf = jax.ShapeDtypeStruct((bsz, t, d), BF16)
    out_shape = [out_bf] * 7 + [jax.ShapeDtypeStruct((bsz, t // WKV_L, d), F32)]
    out_specs = [row_spec] * 7 + [pl.BlockSpec((None, tm // WKV_L, d), lambda b, i: (b, i, 0))]
    return pl.pallas_call(
        _rwkv_proj_kernel,
        grid=(bsz, nt),
        in_specs=[row_spec, prev_spec] + [full(w) for w in weights],
        out_specs=out_specs,
        out_shape=out_shape,
        compiler_params=pltpu.CompilerParams(dimension_semantics=("parallel", "parallel"),
                                             vmem_limit_bytes=VMEM_LIMIT),
        name="rwkv_proj",
    )(x, x, *weights)


def _wkv_kernel(rt_ref, kt_ref, bt_ref, at_ref, v_ref, gl_ref, y_ref, st_ref, t_scr, rb_scr, x1_scr):
    tb, d = rt_ref.shape
    L, W = WKV_L, MXU_DIM
    npack = d // W
    row = _iota((L, W), 0)
    sidx = _iota((L, W), 1) % L
    strict = sidx < row
    incl = sidx <= row
    eye = sidx == row
    r2 = _iota((W, W), 0)
    c2 = _iota((W, W), 1)
    bd = (r2 // L) == (c2 // L)
    diag = r2 == c2

    def stack(xb):
        return jnp.where(bd, jnp.concatenate([xb] * PACK, axis=0), jnp.zeros((), xb.dtype))

    @pl.when(pl.program_id(1) == 0)
    def _():
        st_ref[...] = jnp.zeros_like(st_ref)

    def rows(c):
        return pl.ds(pl.multiple_of(c * L, L), L)

    def prepare(c, pk, slot):
        sl = rows(c)
        ln = slice(pk * W, (pk + 1) * W)
        rt = rt_ref[sl, ln]
        kt = kt_ref[sl, ln]
        bt = bt_ref[sl, ln]
        at = at_ref[sl, ln]
        v = v_ref[sl, ln]
        gmat = _dot_nt(jnp.concatenate([at, rt], axis=0),
                       jnp.concatenate([stack(bt), stack(kt)], axis=0))
        yield
        ab = gmat[:L, :W]
        ak = gmat[:L, W:]
        rb = gmat[L:, :W]
        rk = gmat[L:, W:]
        n1 = jnp.where(strict, ab, 0.0)
        tm_ = jnp.where(eye, 1.0, 0.0) + n1
        cur = n1.astype(BF16)
        cur = _dot(cur, stack(cur)).astype(BF16)
        res = _dot(jnp.concatenate([jnp.where(strict, ak, 0.0), jnp.where(incl, rk, 0.0)], axis=0), stack(v))
        yield
        for _ in range(4):
            nres = _dot(jnp.concatenate([cur, tm_.astype(BF16)], axis=0), stack(cur))
            tm_ = tm_ + nres[L:]
            cur = nres[:L].astype(BF16)
            yield
        tm_ = tm_ + _dot(tm_, stack(cur))
        yield
        t_scr[slot, pk] = tm_.astype(BF16)
        rb_scr[slot, pk] = jnp.where(incl, rb, 0.0).astype(BF16)
        x1_scr[slot, pk] = res

    def advance(c, pk, slot):
        sl = rows(c)
        ln = slice(pk * W, (pk + 1) * W)
        tm_ = t_scr[slot, pk]
        rbi = rb_scr[slot, pk]
        x1r = x1_scr[slot, pk]
        rt = rt_ref[sl, ln]
        kt = kt_ref[sl, ln]
        bt = bt_ref[sl, ln]
        at = at_ref[sl, ln]
        v = v_ref[sl, ln]
        gl = gl_ref[pl.ds(c, 1), ln]
        st = st_ref[pk]
        ra = _dot(jnp.concatenate([rt, at], axis=0), st)
        yield
        ub = _dot(tm_, stack((x1r[:L] + ra[L:]).astype(BF16))).astype(BF16)
        yield
        yr = _dot(rbi, stack(ub))
        bh = (bt.astype(F32) * gl).astype(BF16)
        kh = (kt.astype(F32) * gl).astype(BF16)
        upd = _dot_tn(jnp.concatenate([bh, kh], axis=0), jnp.concatenate([ub, v], axis=0))
        yield
        glcol = jnp.sum(jnp.where(diag, gl, 0.0), axis=1, keepdims=True)
        st_ref[pk] = glcol * st + jnp.where(bd, upd, 0.0)
        y_ref[sl, ln] = (ra[:L] + yr + x1r[L:]).astype(y_ref.dtype)

    def run(*gens):
        for _ in itertools.zip_longest(*gens):
            pass

    npair = tb // (2 * L)
    packs = range(npack)

    def advance2(c, s0, s1, pk):
        return itertools.chain(advance(c, pk, s0), advance(c + 1, pk, s1))

    run(*([prepare(0, pk, 0) for pk in packs] + [prepare(1, pk, 1) for pk in packs]))

    def body(i, carry):
        c = 2 * i
        s0 = c % 4
        run(*([advance2(c, s0, s0 + 1, pk) for pk in packs]
              + [prepare(c + 2, pk, (s0 + 2) % 4) for pk in packs]
              + [prepare(c + 3, pk, (s0 + 3) % 4) for pk in packs]))
        return carry

    lax.fori_loop(0, npair - 1, body, 0)
    c_last = 2 * (npair - 1)
    run(*[advance2(c_last, c_last % 4, c_last % 4 + 1, pk) for pk in packs])


def _wkv(rt, kt, bt, at, v, gl, tb=1024):
    bsz, t, d = rt.shape
    npack = d // MXU_DIM
    seq_spec = pl.BlockSpec((None, tb, d), lambda b, i: (b, i, 0))
    gl_spec = pl.BlockSpec((None, tb // WKV_L, d), lambda b, i: (b, i, 0))
    return pl.pallas_call(
        _wkv_kernel,
        grid=(bsz, t // tb),
        in_specs=[seq_spec] * 5 + [gl_spec],
        out_specs=seq_spec,
        out_shape=jax.ShapeDtypeStruct((bsz, t, d), BF16),
        scratch_shapes=[pltpu.VMEM((npack, MXU_DIM, MXU_DIM), F32),
                        pltpu.VMEM((4, npack, WKV_L, MXU_DIM), BF16),
                        pltpu.VMEM((4, npack, WKV_L, MXU_DIM), BF16),
                        pltpu.VMEM((4, npack, 2 * WKV_L, MXU_DIM), F32)],
        compiler_params=pltpu.CompilerParams(dimension_semantics=("parallel", "arbitrary"),
                                             vmem_limit_bytes=VMEM_LIMIT),
        name="wkv7",
    )(rt, kt, bt, at, v, gl)


def _post_block_kernel(*refs, rwkv, project, sub_rows):
    it = iter(refs)
    x_ref = next(it)
    y_ref = next(it)
    if rwkv:
        bonus_ref, g_ref, lnxg_ref, lnxb_ref = next(it), next(it), next(it), next(it)
    wo_ref, ln_ref, w1_ref, w2_ref = next(it), next(it), next(it), next(it)
    if project:
        wq_ref, wkv_ref = next(it), next(it)
    out_ref = next(it)
    if project:
        q_ref, k_ref, v_ref = next(it), next(it), next(it)

    tm, d = x_ref.shape
    ln = ln_ref[...]
    ones_bd = _head_block_ones() if rwkv else None

    def sub_tile(rs):
        x = x_ref[rs, :]
        if rwkv:
            y = y_ref[rs, :].astype(F32)
            mean = _head_sum_bcast(y, ones_bd) * (1.0 / HEAD)
            yc = y - mean
            var = _head_sum_bcast(yc * yc, ones_bd) * (1.0 / HEAD)
            yn = yc * lax.rsqrt(var + GN_EPS) * lnxg_ref[...] + lnxb_ref[...]
            mixed = ((yn + bonus_ref[rs, :].astype(F32)) * g_ref[rs, :].astype(F32)).astype(BF16)
        else:
            mixed = y_ref[rs, :]
        h = _dot(mixed, wo_ref[...])
        yield
        x1 = _layer_norm(ALPHA * x + h, ln[0:1, :], ln[1:2, :], LN_EPS)
        yield
        hid = jnp.maximum(_dot(x1, w1_ref[...]), 0.0)
        h2 = _dot(hid * hid, w2_ref[...])
        yield
        x2 = _layer_norm(ALPHA * x1 + h2, ln[2:3, :], ln[3:4, :], LN_EPS)
        out_ref[rs, :] = x2
        yield
        if project:
            x2b = x2.astype(BF16)
            q_ref[rs, :] = (_dot(x2b, wq_ref[...]) * (HEAD ** -0.5 * LOG2E)).astype(BF16)
            kv = _dot(x2b, wkv_ref[...])
            k_ref[rs, :] = kv[:, :d].astype(BF16)
            v_ref[rs, :] = kv[:, d:].astype(BF16)

    _run_staggered([sub_tile(slice(r, r + sub_rows)) for r in range(0, tm, sub_rows)])


def _post_block(x, y, rw, w_o, ln, w1, w2, proj, sub_rows=256):
    bsz, t, d = x.shape
    rwkv = rw is not None
    project = proj is not None
    tm = 2 * sub_rows
    row_spec = pl.BlockSpec((None, tm, d), lambda b, i: (b, i, 0))

    def full(arr):
        nd = arr.ndim
        return pl.BlockSpec(arr.shape, lambda b, i: (0,) * nd, pipeline_mode=pl.Buffered(1))

    args = [x, y]
    specs = [row_spec, row_spec]
    if rwkv:
        args += [rw["bonus"], rw["g"], rw["lnx_g"], rw["lnx_b"]]
        specs += [row_spec, row_spec, full(rw["lnx_g"]), full(rw["lnx_b"])]
    args += [w_o, ln, w1, w2]
    specs += [full(w_o), full(ln), full(w1), full(w2)]
    out_shape = [jax.ShapeDtypeStruct((bsz, t, d), F32)]
    out_specs = [row_spec]
    if project:
        args += [proj["w_q"], proj["w_kv"]]
        specs += [full(proj["w_q"]), full(proj["w_kv"])]
        out_shape += [jax.ShapeDtypeStruct((bsz, t, d), BF16)] * 3
        out_specs += [row_spec] * 3
    return pl.pallas_call(
        functools.partial(_post_block_kernel, rwkv=rwkv, project=project, sub_rows=sub_rows),
        grid=(bsz, t // tm),
        in_specs=specs,
        out_specs=out_specs,
        out_shape=out_shape,
        compiler_params=pltpu.CompilerParams(dimension_semantics=("parallel", "parallel"),
                                             vmem_limit_bytes=VMEM_LIMIT),
        name="post_rwkv" if rwkv else "post_attn",
    )(*args)


def _t5_bucket_np(rel):
    nb = REL_BUCKETS // 2
    max_exact = nb // 2
    n = np.abs(rel)
    thresholds = [int(math.ceil(max_exact * (REL_MAX_DIST / max_exact) ** (m / (nb - max_exact)) - 1e-9))
                  for m in range(1, nb - max_exact)]
    large = max_exact + sum((n >= th).astype(np.int64) for th in thresholds)
    return np.where(rel > 0, nb, 0) + np.where(n < max_exact, n, np.minimum(large, nb - 1))


def _bias_bucket_tables():
    kl = np.arange(KEY_TILE)[:, None]
    qpos = np.arange(2 * Q_BLOCK)[None, :]
    out = np.zeros((2, KEY_TILE, 2 * Q_BLOCK), np.int32)
    out[0] = _t5_bucket_np(kl - KEY_TILE - qpos)
    allowed = (kl // ATT_CHUNK) <= (qpos // ATT_CHUNK)
    out[1] = np.where(allowed, _t5_bucket_np(kl - qpos), -1)
    return out


def _bias_table_kernel(bucket_ref, relb_ref, out_ref):
    h = pl.program_id(0)
    bucket = bucket_ref[...]
    acc = jnp.where(bucket < 0, NEG_BIG, 0.0).astype(F32)
    for b in range(REL_BUCKETS):
        acc = jnp.where(bucket == b, relb_ref[b, h], acc)
    out_ref[...] = jnp.where(bucket < 0, NEG_BIG, acc * LOG2E)


def _bias_tables(rel_bias):
    nh = rel_bias.shape[1]
    buckets = jnp.asarray(_bias_bucket_tables())
    return pl.pallas_call(
        _bias_table_kernel,
        grid=(nh,),
        in_specs=[pl.BlockSpec(buckets.shape, lambda h: (0, 0, 0)),
                  pl.BlockSpec(memory_space=pltpu.SMEM)],
        out_specs=pl.BlockSpec((None,) + buckets.shape, lambda h: (h, 0, 0, 0)),
        out_shape=jax.ShapeDtypeStruct((nh,) + buckets.shape, F32),
        name="t5_bias_tables",
    )(buckets, rel_bias)


def _diff_attn_kernel(q_ref, k_ref, v_ref, tab_ref, relb_ref, lam_ref, sg_ref, o_ref, s_scr, p_scr, ve_scr, *,
                      lambda_init):
    h = pl.program_id(1)
    t = q_ref.shape[0]
    qb, kt_, dh = Q_BLOCK, KEY_TILE, HEAD
    c_far = relb_ref[REL_BUCKETS // 2 - 1, h] * LOG2E
    lam = lam_ref[...]
    lam_full = (jnp.exp(jnp.sum(lam[0:1, :] * lam[1:2, :], axis=-1, keepdims=True))
                - jnp.exp(jnp.sum(lam[2:3, :] * lam[3:4, :], axis=-1, keepdims=True)) + lambda_init)
    qp = 2 * qb
    dv = 2 * dh
    lane = _iota((qp, 2 * dh), 1)
    sg = sg_ref[...] * (1.0 - lambda_init)
    prev_tab = jnp.concatenate([tab_ref[0]] * 2, axis=1)
    last_tab = jnp.concatenate([tab_ref[1]] * 2, axis=1)
    ve_scr[:dv, :] = jnp.transpose(v_ref[...].astype(F32)).astype(BF16)
    ve_scr[dv:, :] = jnp.ones((ve_scr.shape[0] - dv, t), BF16)

    npairs = t // kt_
    col_max = {}
    dyn0 = jnp.minimum(pl.program_id(0), 0)

    def logits(jl):
        slot = jl % 2 + dyn0
        q = q_ref[jl * qp:(jl + 1) * qp, :]
        zero = jnp.zeros((), q.dtype)
        qm = jnp.concatenate([jnp.where(lane < dh, q, zero), jnp.where(lane >= dh, q, zero)], axis=0)
        mx_far = jnp.full((1, 2 * qp), NEG_BIG, F32)
        mx_near = jnp.full((1, 2 * qp), NEG_BIG, F32)
        for j in range(jl + 1):
            ks = slice(j * kt_, (j + 1) * kt_)
            s = _dot_nt(k_ref[ks, :], qm)
            if j >= jl - 1:
                s = s + (last_tab if j == jl else prev_tab)
                mx_near = jnp.maximum(mx_near, jnp.max(s, axis=0, keepdims=True))
            else:
                mx_far = jnp.maximum(mx_far, jnp.max(s, axis=0, keepdims=True))
            s_scr[slot, ks, :] = s
            yield
        col_max[jl] = jnp.maximum(mx_near, mx_far + c_far)

    def outputs(jl):
        slot = jl % 2 + dyn0
        m = col_max.pop(jl)
        m_far = m - c_far
        for j in range(jl + 1):
            ks = slice(j * kt_, (j + 1) * kt_)
            p_scr[ks, :] = jnp.exp2(s_scr[slot, ks, :] - (m if j >= jl - 1 else m_far)).astype(BF16)
            yield
        kend = (jl + 1) * kt_
        acc = jnp.dot(ve_scr[:, :kend], p_scr[:kend, :], preferred_element_type=F32)
        pv = acc[:dv, :] / acc[dv:dv + 1, :]
        o = pv[:, :qp] - lam_full * pv[:, qp:]
        o = o * lax.rsqrt(jnp.mean(o * o, axis=0, keepdims=True) + SUBLN_EPS)
        o_ref[jl * qp:(jl + 1) * qp, :] = jnp.transpose(o * sg).astype(o_ref.dtype)

    def run(*gens):
        for _ in itertools.zip_longest(*gens):
            pass

    run(logits(npairs - 1))
    for jl in reversed(range(npairs)):
        run(*([outputs(jl)] + ([logits(jl - 1)] if jl > 0 else [])))


def _diff_attn(q, k, v, tabs, rel_bias, lam, subln_g, lambda_init):
    bsz, t, d = q.shape
    nh = d // (2 * HEAD)
    seq_spec = pl.BlockSpec((None, t, 2 * HEAD), lambda b, h: (b, 0, h))
    return pl.pallas_call(
        functools.partial(_diff_attn_kernel, lambda_init=lambda_init),
        grid=(bsz, nh),
        in_specs=[seq_spec, seq_spec, seq_spec,
                  pl.BlockSpec((None,) + tabs.shape[1:], lambda b, h: (h, 0, 0, 0)),
                  pl.BlockSpec(memory_space=pltpu.SMEM),
                  pl.BlockSpec(lam.shape, lambda b, h: (0, 0)),
                  pl.BlockSpec(subln_g.shape, lambda b, h: (0, 0))],
        out_specs=seq_spec,
        out_shape=jax.ShapeDtypeStruct((bsz, t, d), BF16),
        scratch_shapes=[pltpu.VMEM((2, t, 4 * Q_BLOCK), F32),
                        pltpu.VMEM((t, 4 * Q_BLOCK), BF16),
                        pltpu.VMEM((2 * HEAD + 16, t), BF16)],
        compiler_params=pltpu.CompilerParams(dimension_semantics=("parallel", "parallel"),
                                             vmem_limit_bytes=VMEM_LIMIT),
        name="diff_attn",
    )(q, k, v, tabs, rel_bias, lam, subln_g)


def kernel(x, a_mu, a_w_r, a_w_k, a_w_v, a_w_o, a_w0, a_w1, a_w2, a_a0, a_a1, a_a2, a_g1, a_g2, a_k_k, a_k_a,
           a_r_k, a_lnx_g, a_lnx_b, b_w_kv, b_w_q, b_lam, b_subln_g, b_w_o, rel_bias, mlp_w1, mlp_w2, ln_g, ln_b):
    bsz, t, d = x.shape
    assert d % MXU_DIM == 0 and t % 512 == 0
    bf = lambda w: w.astype(BF16)
    row = lambda w: w.reshape(1, d)

    p = dict(mu=a_mu[0], w_r=bf(a_w_r[0]), w_k=bf(a_w_k[0]), w_v=bf(a_w_v[0]), w0=row(a_w0[0]),
             w1=bf(a_w1[0]), w2=bf(a_w2[0]), a0=row(a_a0[0]), a1=bf(a_a1[0]), a2=bf(a_a2[0]),
             g1=bf(a_g1[0]), g2=bf(a_g2[0]), k_k=row(a_k_k[0]), k_a=row(a_k_a[0]), r_k=row(a_r_k[0]))
    rt, kt, bt, at, v, bonus, g, gl = _rwkv_proj(x, p)
    y = _wkv(rt, kt, bt, at, v, gl)
    ln0 = jnp.stack([ln_g[0, 0], ln_b[0, 0], ln_g[0, 1], ln_b[0, 1]])
    x, q, k_sh, v_sh = _post_block(
        x, y, dict(bonus=bonus, g=g, lnx_g=row(a_lnx_g[0]), lnx_b=row(a_lnx_b[0])),
        bf(a_w_o[0]), ln0, bf(mlp_w1[0]), bf(mlp_w2[0]), dict(w_q=bf(b_w_q[0]), w_kv=bf(b_w_kv)))

    lambda_init = 0.8 - 0.6 * math.exp(-0.3 * 1)
    tabs = _bias_tables(rel_bias)
    o = _diff_attn(q, k_sh, v_sh, tabs, rel_bias, b_lam[0], b_subln_g[0].reshape(2 * HEAD, 1), lambda_init)
    ln1 = jnp.stack([ln_g[1, 0], ln_b[1, 0], ln_g[1, 1], ln_b[1, 1]])
    (x,) = _post_block(x, o, None, bf(b_w_o[0]), ln1, bf(mlp_w1[1]), bf(mlp_w2[1]), None)
    return x
```

```python
import functools
import itertools
import math

import numpy as np
import jax
import jax.numpy as jnp
from jax import lax
from jax.experimental import pallas as pl
from jax.experimental.pallas import tpu as pltpu

F32 = jnp.float32
BF16 = jnp.bfloat16

DEPTH = 2
ALPHA = (2.0 * DEPTH) ** 0.25
HEAD = 64
GN_EPS = 64e-5
SUBLN_EPS = 1e-5
LN_EPS = 1e-5
REL_BUCKETS = 32
REL_MAX_DIST = 128
ATT_CHUNK = 64
Q_BLOCK = 128

LANES = 128
MXU_DIM = 256
WKV_L = 64
PACK = MXU_DIM // HEAD
KEY_TILE = 256
NEG_BIG = -1e30
LOG2E = math.log2(math.e)
VMEM_LIMIT = 56 * 1024 * 1024


def _dot(a, b):
    return jnp.dot(a.astype(BF16), b.astype(BF16), preferred_element_type=F32)


def _dot_nt(a, b):
    return lax.dot_general(a.astype(BF16), b.astype(BF16), (((1,), (1,)), ((), ())),
                           preferred_element_type=F32)


def _dot_tn(a, b):
    return lax.dot_general(a.astype(BF16), b.astype(BF16), (((0,), (0,)), ((), ())),
                           preferred_element_type=F32)


def _iota(shape, dim):
    return lax.broadcasted_iota(jnp.int32, shape, dim)


def _head_block_ones():
    r = _iota((MXU_DIM, MXU_DIM), 0) // HEAD
    c = _iota((MXU_DIM, MXU_DIM), 1) // HEAD
    return jnp.where(r == c, 1.0, 0.0).astype(BF16)


def _head_sum_bcast(x, ones_bd):
    d = x.shape[1]
    parts = [_dot(x[:, p:p + MXU_DIM], ones_bd) for p in range(0, d, MXU_DIM)]
    return jnp.concatenate(parts, axis=1)


def _run_staggered(gens):
    gens = list(gens)
    live = []
    while gens or live:
        if gens:
            live.append(gens.pop(0))
        for g in list(live):
            if next(g, StopIteration) is StopIteration:
                live.remove(g)


def _layer_norm(x, g, b, eps):
    mu = jnp.mean(x, axis=-1, keepdims=True)
    xc = x - mu
    var = jnp.mean(xc * xc, axis=-1, keepdims=True)
    return xc * lax.rsqrt(var + eps) * g + b


def _rwkv_proj_kernel(x_ref, xp_ref, mu_ref, wr_ref, wk_ref, wv_ref, w0_ref, w1_ref, w2_ref,
                      a0_ref, a1_ref, a2_ref, g1_ref, g2_ref, kk_ref, ka_ref, rk_ref,
                      rt_ref, kt_ref, bt_ref, at_ref, v_ref, bonus_ref, g_ref, gl_ref):
    i = pl.program_id(1)
    x = x_ref[...]
    tm, d = x.shape
    prev_row = jnp.where(i == 0, 0.0, xp_ref[7:8, :])
    row = _iota((tm, d), 0)
    xprev = jnp.where(row == 0, prev_row, pltpu.roll(x, 1, 0))
    xx = xprev - x
    mu = mu_ref[...]

    def mix(j):
        return (x + xx * mu[j:j + 1, :]).astype(BF16)

    r = _dot(mix(0), wr_ref[...])
    wraw = w0_ref[...] + _dot(jnp.tanh(_dot(mix(1), w1_ref[...])), w2_ref[...])
    k = _dot(mix(2), wk_ref[...])
    v = _dot(mix(3), wv_ref[...])
    a = jax.nn.sigmoid(a0_ref[...] + _dot(_dot(mix(4), a1_ref[...]), a2_ref[...]))
    g = _dot(jax.nn.sigmoid(_dot(mix(5), g1_ref[...])), g2_ref[...])

    w = jnp.minimum(wraw, 0.0) - jnp.log(1.0 + jnp.exp(-jnp.abs(wraw))) - 0.5
    ld = -jnp.exp(w)

    ones_bd = _head_block_ones()
    kk = k * kk_ref[...]
    kk = kk * jnp.minimum(lax.rsqrt(_head_sum_bcast(kk * kk, ones_bd)), 1e12)
    k = k * (1.0 + (a - 1.0) * ka_ref[...])
    bonus = _head_sum_bcast(r * k * rk_ref[...], ones_bd) * v

    tr = _iota((MXU_DIM, MXU_DIM), 0)
    tc = _iota((MXU_DIM, MXU_DIM), 1)
    tri = jnp.where((tr // WKV_L == tc // WKV_L) & (tc <= tr), 1.0, 0.0).astype(BF16)
    ld_hi = ld.astype(BF16)
    ld_lo = (ld - ld_hi.astype(F32)).astype(BF16)
    cs = jnp.concatenate(
        [jnp.dot(tri, ld_hi[p:p + MXU_DIM, :], preferred_element_type=F32)
         + jnp.dot(tri, ld_lo[p:p + MXU_DIM, :], preferred_element_type=F32)
         for p in range(0, tm, MXU_DIM)], axis=0)

    gam = jnp.exp(cs)
    igam = jnp.exp(-cs)
    rt_ref[...] = (r * gam).astype(BF16)
    kt_ref[...] = (k * igam).astype(BF16)
    bt_ref[...] = (kk * a * igam).astype(BF16)
    at_ref[...] = (-kk * jnp.exp(cs - ld)).astype(BF16)
    v_ref[...] = v.astype(BF16)
    bonus_ref[...] = bonus.astype(BF16)
    g_ref[...] = g.astype(BF16)
    gl_ref[...] = jnp.exp(cs.reshape(tm // WKV_L, WKV_L, d)[:, WKV_L - 1, :])


def _rwkv_proj(x, p, tm=512):
    bsz, t, d = x.shape
    nt = t // tm
    row_spec = pl.BlockSpec((None, tm, d), lambda b, i: (b, i, 0))
    prev_spec = pl.BlockSpec((None, 8, d), lambda b, i: (b, jnp.maximum(i * (tm // 8) - 1, 0), 0))

    def full(arr):
        nd = arr.ndim
        return pl.BlockSpec(arr.shape, lambda b, i: (0,) * nd, pipeline_mode=pl.Buffered(1))

    weights = [p["mu"], p["w_r"], p["w_k"], p["w_v"], p["w0"], p["w1"], p["w2"], p["a0"], p["a1"], p["a2"],
               p["g1"], p["g2"], p["k_k"], p["k_a"], p["r_k"]]
    out_bf = jax.ShapeDtypeStruct((bsz, t, d), BF16)
    out_shape = [out_bf] * 7 + [jax.ShapeDtypeStruct((bsz, t // WKV_L, d), F32)]
    out_specs = [row_spec] * 7 + [pl.BlockSpec((None, tm // WKV_L, d), lambda b, i: (b, i, 0))]
    return pl.pallas_call(
        _rwkv_proj_kernel,
        grid=(bsz, nt),
        in_specs=[row_spec, prev_spec] + [full(w) for w in weights],
        out_specs=out_specs,
        out_shape=out_shape,
        compiler_params=pltpu.CompilerParams(dimension_semantics=("parallel", "parallel"),
                                             vmem_limit_bytes=VMEM_LIMIT),
        name="rwkv_proj",
    )(x, x, *weights)


def _wkv_kernel(rt_ref, kt_ref, bt_ref, at_ref, v_ref, gl_ref, y_ref, st_ref, t_scr, rb_scr, x1_scr):
    tb, d = rt_ref.shape
    L, W = WKV_L, MXU_DIM
    npack = d // W
    row = _iota((L, W), 0)
    sidx = _iota((L, W), 1) % L
    strict = sidx < row
    incl = sidx <= row
    eye = sidx == row
    r2 = _iota((W, W), 0)
    c2 = _iota((W, W), 1)
    bd = (r2 // L) == (c2 // L)
    diag = r2 == c2

    def stack(xb):
        return jnp.where(bd, jnp.concatenate([xb] * PACK, axis=0), jnp.zeros((), xb.dtype))

    @pl.when(pl.program_id(1) == 0)
    def _():
        st_ref[...] = jnp.zeros_like(st_ref)

    def rows(c):
        return pl.ds(pl.multiple_of(c * L, L), L)

    def prepare(c, pk, slot):
        sl = rows(c)
        ln = slice(pk * W, (pk + 1) * W)
        rt = rt_ref[sl, ln]
        kt = kt_ref[sl, ln]
        bt = bt_ref[sl, ln]
        at = at_ref[sl, ln]
        v = v_ref[sl, ln]
        gmat = _dot_nt(jnp.concatenate([at, rt], axis=0),
                       jnp.concatenate([stack(bt), stack(kt)], axis=0))
        yield
        ab = gmat[:L, :W]
        ak = gmat[:L, W:]
        rb = gmat[L:, :W]
        rk = gmat[L:, W:]
        n1 = jnp.where(strict, ab, 0.0)
        tm_ = jnp.where(eye, 1.0, 0.0) + n1
        cur = n1.astype(BF16)
        cur = _dot(cur, stack(cur)).astype(BF16)
        res = _dot(jnp.concatenate([jnp.where(strict, ak, 0.0), jnp.where(incl, rk, 0.0)], axis=0), stack(v))
        yield
        for _ in range(4):
            nres = _dot(jnp.concatenate([cur, tm_.astype(BF16)], axis=0), stack(cur))
            tm_ = tm_ + nres[L:]
            cur = nres[:L].astype(BF16)
            yield
        tm_ = tm_ + _dot(tm_, stack(cur))
        yield
        t_scr[slot, pk] = tm_.astype(BF16)
        rb_scr[slot, pk] = jnp.where(incl, rb, 0.0).astype(BF16)
        x1_scr[slot, pk] = res

    def advance(c, pk, slot):
        sl = rows(c)
        ln = slice(pk * W, (pk + 1) * W)
        tm_ = t_scr[slot, pk]
        rbi = rb_scr[slot, pk]
        x1r = x1_scr[slot, pk]
        rt = rt_ref[sl, ln]
        kt = kt_ref[sl, ln]
        bt = bt_ref[sl, ln]
        at = at_ref[sl, ln]
        v = v_ref[sl, ln]
        gl = gl_ref[pl.ds(c, 1), ln]
        st = st_ref[pk]
        ra = _dot(jnp.concatenate([rt, at], axis=0), st)
        yield
        ub = _dot(tm_, stack((x1r[:L] + ra[L:]).astype(BF16))).astype(BF16)
        yield
        yr = _dot(rbi, stack(ub))
        bh = (bt.astype(F32) * gl).astype(BF16)
        kh = (kt.astype(F32) * gl).astype(BF16)
        upd = _dot_tn(jnp.concatenate([bh, kh], axis=0), jnp.concatenate([ub, v], axis=0))
        yield
        glcol = jnp.sum(jnp.where(diag, gl, 0.0), axis=1, keepdims=True)
        st_ref[pk] = glcol * st + jnp.where(bd, upd, 0.0)
        y_ref[sl, ln] = (ra[:L] + yr + x1r[L:]).astype(y_ref.dtype)

    def run(*gens):
        for _ in itertools.zip_longest(*gens):
            pass

    npair = tb // (2 * L)
    packs = range(npack)

    def advance2(c, s0, s1, pk):
        return itertools.chain(advance(c, pk, s0), advance(c + 1, pk, s1))

    run(*([prepare(0, pk, 0) for pk in packs] + [prepare(1, pk, 1) for pk in packs]))

    def body(i, carry):
        c = 2 * i
        s0 = c % 4
        run(*([advance2(c, s0, s0 + 1, pk) for pk in packs]
              + [prepare(c + 2, pk, (s0 + 2) % 4) for pk in packs]
              + [prepare(c + 3, pk, (s0 + 3) % 4) for pk in packs]))
        return carry

    lax.fori_loop(0, npair - 1, body, 0)
    c_last = 2 * (npair - 1)
    run(*[advance2(c_last, c_last % 4, c_last % 4 + 1, pk) for pk in packs])


def _wkv(rt, kt, bt, at, v, gl, tb=1024):
    bsz, t, d = rt.shape
    npack = d // MXU_DIM
    seq_spec = pl.BlockSpec((None, tb, d), lambda b, i: (b, i, 0))
    gl_spec = pl.BlockSpec((None, tb // WKV_L, d), lambda b, i: (b, i, 0))
    return pl.pallas_call(
        _wkv_kernel,
        grid=(bsz, t // tb),
        in_specs=[seq_spec] * 5 + [gl_spec],
        out_specs=seq_spec,
        out_shape=jax.ShapeDtypeStruct((bsz, t, d), BF16),
        scratch_shapes=[pltpu.VMEM((npack, MXU_DIM, MXU_DIM), F32),
                        pltpu.VMEM((4, npack, WKV_L, MXU_DIM), BF16),
                        pltpu.VMEM((4, npack, WKV_L, MXU_DIM), BF16),
                        pltpu.VMEM((4, npack, 2 * WKV_L, MXU_DIM), F32)],
        compiler_params=pltpu.CompilerParams(dimension_semantics=("parallel", "arbitrary"),
                                             vmem_limit_bytes=VMEM_LIMIT),
        name="wkv7",
    )(rt, kt, bt, at, v, gl)


def _post_block_kernel(*refs, rwkv, project, sub_rows):
    it = iter(refs)
    x_ref = next(it)
    y_ref = next(it)
    if rwkv:
        bonus_ref, g_ref, lnxg_ref, lnxb_ref = next(it), next(it), next(it), next(it)
    wo_ref, ln_ref, w1_ref, w2_ref = next(it), next(it), next(it), next(it)
    if project:
        wq_ref, wkv_ref = next(it), next(it)
    out_ref = next(it)
    if project:
        q_ref, k_ref, v_ref = next(it), next(it), next(it)

    tm, d = x_ref.shape
    ln = ln_ref[...]
    ones_bd = _head_block_ones() if rwkv else None

    def sub_tile(rs):
        x = x_ref[rs, :]
        if rwkv:
            y = y_ref[rs, :].astype(F32)
            mean = _head_sum_bcast(y, ones_bd) * (1.0 / HEAD)
            yc = y - mean
            var = _head_sum_bcast(yc * yc, ones_bd) * (1.0 / HEAD)
            yn = yc * lax.rsqrt(var + GN_EPS) * lnxg_ref[...] + lnxb_ref[...]
            mixed = ((yn + bonus_ref[rs, :].astype(F32)) * g_ref[rs, :].astype(F32)).astype(BF16)
        else:
            mixed = y_ref[rs, :]
        h = _dot(mixed, wo_ref[...])
        yield
        x1 = _layer_norm(ALPHA * x + h, ln[0:1, :], ln[1:2, :], LN_EPS)
        yield
        hid = jnp.maximum(_dot(x1, w1_ref[...]), 0.0)
        h2 = _dot(hid * hid, w2_ref[...])
        yield
        x2 = _layer_norm(ALPHA * x1 + h2, ln[2:3, :], ln[3:4, :], LN_EPS)
        out_ref[rs, :] = x2
        yield
        if project:
            x2b = x2.astype(BF16)
            q_ref[rs, :] = (_dot(x2b, wq_ref[...]) * (HEAD ** -0.5 * LOG2E)).astype(BF16)
            kv = _dot(x2b, wkv_ref[...])
            k_ref[rs, :] = kv[:, :d].astype(BF16)
            v_ref[rs, :] = kv[:, d:].astype(BF16)

    _run_staggered([sub_tile(slice(r, r + sub_rows)) for r in range(0, tm, sub_rows)])


def _post_block(x, y, rw, w_o, ln, w1, w2, proj, sub_rows=256):
    bsz, t, d = x.shape
    rwkv = rw is not None
    project = proj is not None
    tm = 2 * sub_rows
    row_spec = pl.BlockSpec((None, tm, d), lambda b, i: (b, i, 0))

    def full(arr):
        nd = arr.ndim
        return pl.BlockSpec(arr.shape, lambda b, i: (0,) * nd, pipeline_mode=pl.Buffered(1))

    args = [x, y]
    specs = [row_spec, row_spec]
    if rwkv:
        args += [rw["bonus"], rw["g"], rw["lnx_g"], rw["lnx_b"]]
        specs += [row_spec, row_spec, full(rw["lnx_g"]), full(rw["lnx_b"])]
    args += [w_o, ln, w1, w2]
    specs += [full(w_o), full(ln), full(w1), full(w2)]
    out_shape = [jax.ShapeDtypeStruct((bsz, t, d), F32)]
    out_specs = [row_spec]
    if project:
        args += [proj["w_q"], proj["w_kv"]]
        specs += [full(proj["w_q"]), full(proj["w_kv"])]
        out_shape += [jax.ShapeDtypeStruct((bsz, t, d), BF16)] * 3
        out_specs += [row_spec] * 3
    return pl.pallas_call(
        functools.partial(_post_block_kernel, rwkv=rwkv, project=project, sub_rows=sub_rows),
        grid=(bsz, t // tm),
        in_specs=specs,
        out_specs=out_specs,
        out_shape=out_shape,
        compiler_params=pltpu.CompilerParams(dimension_semantics=("parallel", "parallel"),
                                             vmem_limit_bytes=VMEM_LIMIT),
        name="post_rwkv" if rwkv else "post_attn",
    )(*args)


def _t5_bucket_np(rel):
    nb = REL_BUCKETS // 2
    max_exact = nb // 2
    n = np.abs(rel)
    thresholds = [int(math.ceil(max_exact * (REL_MAX_DIST / max_exact) ** (m / (nb - max_exact)) - 1e-9))
                  for m in range(1, nb - max_exact)]
    large = max_exact + sum((n >= th).astype(np.int64) for th in thresholds)
    return np.where(rel > 0, nb, 0) + np.where(n < max_exact, n, np.minimum(large, nb - 1))


def _bias_bucket_tables():
    kl = np.arange(KEY_TILE)[:, None]
    qpos = np.arange(2 * Q_BLOCK)[None, :]
    out = np.zeros((2, KEY_TILE, 2 * Q_BLOCK), np.int32)
    out[0] = _t5_bucket_np(kl - KEY_TILE - qpos)
    allowed = (kl // ATT_CHUNK) <= (qpos // ATT_CHUNK)
    out[1] = np.where(allowed, _t5_bucket_np(kl - qpos), -1)
    return out


def _bias_table_kernel(bucket_ref, relb_ref, out_ref):
    h = pl.program_id(0)
    bucket = bucket_ref[...]
    acc = jnp.where(bucket < 0, NEG_BIG, 0.0).astype(F32)
    for b in range(REL_BUCKETS):
        acc = jnp.where(bucket == b, relb_ref[b, h], acc)
    out_ref[...] = jnp.where(bucket < 0, NEG_BIG, acc * LOG2E)


def _bias_tables(rel_bias):
    nh = rel_bias.shape[1]
    buckets = jnp.asarray(_bias_bucket_tables())
    return pl.pallas_call(
        _bias_table_kernel,
        grid=(nh,),
        in_specs=[pl.BlockSpec(buckets.shape, lambda h: (0, 0, 0)),
                  pl.BlockSpec(memory_space=pltpu.SMEM)],
        out_specs=pl.BlockSpec((None,) + buckets.shape, lambda h: (h, 0, 0, 0)),
        out_shape=jax.ShapeDtypeStruct((nh,) + buckets.shape, F32),
        name="t5_bias_tables",
    )(buckets, rel_bias)


def _diff_attn_kernel(q_ref, k_ref, v_ref, tab_ref, relb_ref, lam_ref, sg_ref, o_ref, s_scr, p_scr, ve_scr, *,
                      lambda_init):
    t = q_ref.shape[0]
    qb, kt_, dh = Q_BLOCK, KEY_TILE, HEAD
    dv = 2 * dh
    nhead = q_ref.shape[1] // dv
    h0 = pl.program_id(1) * nhead
    lam = lam_ref[...]
    lam_full = (jnp.exp(jnp.sum(lam[0:1, :] * lam[1:2, :], axis=-1, keepdims=True))
                - jnp.exp(jnp.sum(lam[2:3, :] * lam[3:4, :], axis=-1, keepdims=True)) + lambda_init)
    qp = 2 * qb
    lane = _iota((qp, dv), 1)
    sg = sg_ref[...] * (1.0 - lambda_init)
    for hh in range(nhead):
        ve_scr[hh, :dv, :] = jnp.transpose(v_ref[:, hh * dv:(hh + 1) * dv].astype(F32)).astype(BF16)
        ve_scr[hh, dv:, :] = jnp.ones((ve_scr.shape[1] - dv, t), BF16)

    npairs = t // kt_
    col_max = {}
    dyn0 = jnp.minimum(pl.program_id(0), 0)

    def logits(idx, hh, jl):
        slot = idx % 2 + dyn0
        hs = slice(hh * dv, (hh + 1) * dv)
        c_far = relb_ref[REL_BUCKETS // 2 - 1, h0 + hh] * LOG2E
        q = q_ref[jl * qp:(jl + 1) * qp, hs]
        zero = jnp.zeros((), q.dtype)
        qm = jnp.concatenate([jnp.where(lane < dh, q, zero), jnp.where(lane >= dh, q, zero)], axis=0)
        mx_far = jnp.full((1, 2 * qp), NEG_BIG, F32)
        mx_near = jnp.full((1, 2 * qp), NEG_BIG, F32)
        for j in range(jl + 1):
            ks = slice(j * kt_, (j + 1) * kt_)
            s = _dot_nt(k_ref[ks, hs], qm)
            if j >= jl - 1:
                tab = tab_ref[hh, 1 if j == jl else 0]
                s = s + jnp.concatenate([tab, tab], axis=1)
                mx_near = jnp.maximum(mx_near, jnp.max(s, axis=0, keepdims=True))
            else:
                mx_far = jnp.maximum(mx_far, jnp.max(s, axis=0, keepdims=True))
            s_scr[slot, ks, :] = s
            yield
        col_max[idx] = (jnp.maximum(mx_near, mx_far + c_far), c_far)

    def outputs(idx, hh, jl):
        slot = idx % 2 + dyn0
        m, c_far = col_max.pop(idx)
        m_far = m - c_far
        for j in range(jl + 1):
            ks = slice(j * kt_, (j + 1) * kt_)
            p_scr[ks, :] = jnp.exp2(s_scr[slot, ks, :] - (m if j >= jl - 1 else m_far)).astype(BF16)
            yield
        kend = (jl + 1) * kt_
        acc = jnp.dot(ve_scr[hh, :, :kend], p_scr[:kend, :], preferred_element_type=F32)
        pv = acc[:dv, :] / acc[dv:dv + 1, :]
        o = pv[:, :qp] - lam_full * pv[:, qp:]
        o = o * lax.rsqrt(jnp.mean(o * o, axis=0, keepdims=True) + SUBLN_EPS)
        o_ref[jl * qp:(jl + 1) * qp, hh * dv:(hh + 1) * dv] = jnp.transpose(o * sg).astype(o_ref.dtype)

    def run(*gens):
        for _ in itertools.zip_longest(*gens):
            pass

    items = [(hh, jl) for hh in range(nhead) for jl in reversed(range(npairs))]
    run(logits(0, *items[0]))
    for idx, item in enumerate(items):
        nxt = [logits(idx + 1, *items[idx + 1])] if idx + 1 < len(items) else []
        run(outputs(idx, *item), *nxt)


def _diff_attn(q, k, v, tabs, rel_bias, lam, subln_g, lambda_init, heads_per_step=2):
    bsz, t, d = q.shape
    nh = d // (2 * HEAD)
    hps = heads_per_step
    seq_spec = pl.BlockSpec((None, t, hps * 2 * HEAD), lambda b, h: (b, 0, h))
    return pl.pallas_call(
        functools.partial(_diff_attn_kernel, lambda_init=lambda_init),
        grid=(bsz, nh // hps),
        in_specs=[seq_spec, seq_spec, seq_spec,
                  pl.BlockSpec((hps,) + tabs.shape[1:], lambda b, h: (h, 0, 0, 0)),
                  pl.BlockSpec(memory_space=pltpu.SMEM),
                  pl.BlockSpec(lam.shape, lambda b, h: (0, 0)),
                  pl.BlockSpec(subln_g.shape, lambda b, h: (0, 0))],
        out_specs=seq_spec,
        out_shape=jax.ShapeDtypeStruct((bsz, t, d), BF16),
        scratch_shapes=[pltpu.VMEM((2, t, 4 * Q_BLOCK), F32),
                        pltpu.VMEM((t, 4 * Q_BLOCK), BF16),
                        pltpu.VMEM((hps, 2 * HEAD + 16, t), BF16)],
        compiler_params=pltpu.CompilerParams(dimension_semantics=("parallel", "parallel"),
                                             vmem_limit_bytes=VMEM_LIMIT),
        name="diff_attn",
    )(q, k, v, tabs, rel_bias, lam, subln_g)


def kernel(x, a_mu, a_w_r, a_w_k, a_w_v, a_w_o, a_w0, a_w1, a_w2, a_a0, a_a1, a_a2, a_g1, a_g2, a_k_k, a_k_a,
           a_r_k, a_lnx_g, a_lnx_b, b_w_kv, b_w_q, b_lam, b_subln_g, b_w_o, rel_bias, mlp_w1, mlp_w2, ln_g, ln_b):
    bsz, t, d = x.shape
    assert d % MXU_DIM == 0 and t % 512 == 0
    bf = lambda w: w.astype(BF16)
    row = lambda w: w.reshape(1, d)

    p = dict(mu=a_mu[0], w_r=bf(a_w_r[0]), w_k=bf(a_w_k[0]), w_v=bf(a_w_v[0]), w0=row(a_w0[0]),
             w1=bf(a_w1[0]), w2=bf(a_w2[0]), a0=row(a_a0[0]), a1=bf(a_a1[0]), a2=bf(a_a2[0]),
             g1=bf(a_g1[0]), g2=bf(a_g2[0]), k_k=row(a_k_k[0]), k_a=row(a_k_a[0]), r_k=row(a_r_k[0]))
    rt, kt, bt, at, v, bonus, g, gl = _rwkv_proj(x, p)
    y = _wkv(rt, kt, bt, at, v, gl)
    ln0 = jnp.stack([ln_g[0, 0], ln_b[0, 0], ln_g[0, 1], ln_b[0, 1]])
    x, q, k_sh, v_sh = _post_block(
        x, y, dict(bonus=bonus, g=g, lnx_g=row(a_lnx_g[0]), lnx_b=row(a_lnx_b[0])),
        bf(a_w_o[0]), ln0, bf(mlp_w1[0]), bf(mlp_w2[0]), dict(w_q=bf(b_w_q[0]), w_kv=bf(b_w_kv)))

    lambda_init = 0.8 - 0.6 * math.exp(-0.3 * 1)
    tabs = _bias_tables(rel_bias)
    o = _diff_attn(q, k_sh, v_sh, tabs, rel_bias, b_lam[0], b_subln_g[0].reshape(2 * HEAD, 1), lambda_init)
    ln1 = jnp.stack([ln_g[1, 0], ln_b[1, 0], ln_g[1, 1], ln_b[1, 1]])
    (x,) = _post_block(x, o, None, bf(b_w_o[0]), ln1, bf(mlp_w1[1]), bf(mlp_w2[1]), None)
    return x
```

```python
import functools
import itertools
import math

import numpy as np
import jax
import jax.numpy as jnp
from jax import lax
from jax.experimental import pallas as pl
from jax.experimental.pallas import tpu as pltpu

F32 = jnp.float32
BF16 = jnp.bfloat16

DEPTH = 2
ALPHA = (2.0 * DEPTH) ** 0.25
HEAD = 64
GN_EPS = 64e-5
SUBLN_EPS = 1e-5
LN_EPS = 1e-5
REL_BUCKETS = 32
REL_MAX_DIST = 128
ATT_CHUNK = 64
Q_BLOCK = 128

LANES = 128
MXU_DIM = 256
WKV_L = 64
PACK = MXU_DIM // HEAD
KEY_TILE = 256

PROJ_ROWS = 512
POST_SUB_ROWS = 256
WKV_TIME_BLOCK = 1024
ATTN_HEADS_PER_STEP = 2
NEG_BIG = -1e30
LOG2E = math.log2(math.e)
VMEM_LIMIT = 56 * 1024 * 1024


def _dot(a, b):
    return jnp.dot(a.astype(BF16), b.astype(BF16), preferred_element_type=F32)


def _dot_nt(a, b):
    return lax.dot_general(a.astype(BF16), b.astype(BF16), (((1,), (1,)), ((), ())),
                           preferred_element_type=F32)


def _dot_tn(a, b):
    return lax.dot_general(a.astype(BF16), b.astype(BF16), (((0,), (0,)), ((), ())),
                           preferred_element_type=F32)


def _iota(shape, dim):
    return lax.broadcasted_iota(jnp.int32, shape, dim)


def _head_block_ones():
    r = _iota((MXU_DIM, MXU_DIM), 0) // HEAD
    c = _iota((MXU_DIM, MXU_DIM), 1) // HEAD
    return jnp.where(r == c, 1.0, 0.0).astype(BF16)


def _head_sum_bcast(x, ones_bd):
    d = x.shape[1]
    parts = [_dot(x[:, p:p + MXU_DIM], ones_bd) for p in range(0, d, MXU_DIM)]
    return jnp.concatenate(parts, axis=1)


def _run_staggered(gens):
    gens = list(gens)
    live = []
    while gens or live:
        if gens:
            live.append(gens.pop(0))
        for g in list(live):
            if next(g, StopIteration) is StopIteration:
                live.remove(g)


def _layer_norm(x, g, b, eps):
    mu = jnp.mean(x, axis=-1, keepdims=True)
    xc = x - mu
    var = jnp.mean(xc * xc, axis=-1, keepdims=True)
    return xc * lax.rsqrt(var + eps) * g + b


def _rwkv_proj_kernel(x_ref, xp_ref, mu_ref, wr_ref, wk_ref, wv_ref, w0_ref, w1_ref, w2_ref,
                      a0_ref, a1_ref, a2_ref, g1_ref, g2_ref, kk_ref, ka_ref, rk_ref,
                      rt_ref, kt_ref, bt_ref, at_ref, v_ref, bonus_ref, g_ref, gl_ref):
    i = pl.program_id(1)
    x = x_ref[...]
    tm, d = x.shape
    prev_row = jnp.where(i == 0, 0.0, xp_ref[7:8, :])
    row = _iota((tm, d), 0)
    xprev = jnp.where(row == 0, prev_row, pltpu.roll(x, 1, 0))
    xx = xprev - x
    mu = mu_ref[...]

    xr, xw, xk, xv, xa, xg = [(x + xx * mu[j:j + 1, :]).astype(BF16) for j in range(6)]
    hw = jnp.tanh(_dot(xw, w1_ref[...])).astype(BF16)
    ha = _dot(xa, a1_ref[...]).astype(BF16)
    hg = jax.nn.sigmoid(_dot(xg, g1_ref[...])).astype(BF16)

    ones_bd = _head_block_ones()
    tr = _iota((MXU_DIM, MXU_DIM), 0)
    tc = _iota((MXU_DIM, MXU_DIM), 1)
    tri = jnp.where((tr // WKV_L == tc // WKV_L) & (tc <= tr), 1.0, 0.0).astype(BF16)

    def col_group(c0):
        cl = slice(c0, c0 + MXU_DIM)
        r = _dot(xr, wr_ref[:, cl])
        k = _dot(xk, wk_ref[:, cl])
        v = _dot(xv, wv_ref[:, cl])
        wraw = w0_ref[:, cl] + _dot(hw, w2_ref[:, cl])
        a = jax.nn.sigmoid(a0_ref[:, cl] + _dot(ha, a2_ref[:, cl]))
        g = _dot(hg, g2_ref[:, cl])
        yield
        ld = -math.exp(-0.5) * jax.nn.sigmoid(wraw)
        kk = k * kk_ref[:, cl]
        kk = kk * jnp.minimum(lax.rsqrt(_dot(kk * kk, ones_bd)), 1e12)
        k = k * (1.0 + (a - 1.0) * ka_ref[:, cl])
        bonus = _dot(r * k * rk_ref[:, cl], ones_bd) * v
        ld_hi = ld.astype(BF16)
        ld_lo = (ld - ld_hi.astype(F32)).astype(BF16)
        cs = jnp.concatenate(
            [jnp.dot(tri, ld_hi[p:p + MXU_DIM, :], preferred_element_type=F32)
             + jnp.dot(tri, ld_lo[p:p + MXU_DIM, :], preferred_element_type=F32)
             for p in range(0, tm, MXU_DIM)], axis=0)
        yield
        igam = jnp.exp(-cs)
        rt_ref[:, cl] = (r * jnp.exp(cs)).astype(BF16)
        kt_ref[:, cl] = (k * igam).astype(BF16)
        bt_ref[:, cl] = (kk * a * igam).astype(BF16)
        at_ref[:, cl] = (-kk * jnp.exp(cs - ld)).astype(BF16)
        v_ref[:, cl] = v.astype(BF16)
        bonus_ref[:, cl] = bonus.astype(BF16)
        g_ref[:, cl] = g.astype(BF16)
        gl_ref[:, cl] = jnp.exp(cs.reshape(tm // WKV_L, WKV_L, MXU_DIM)[:, WKV_L - 1, :])

    _run_staggered([col_group(c0) for c0 in range(0, d, MXU_DIM)])


def _rwkv_proj(x, p, tm=PROJ_ROWS):
    bsz, t, d = x.shape
    nt = t // tm
    row_spec = pl.BlockSpec((None, tm, d), lambda b, i: (b, i, 0))
    prev_spec = pl.BlockSpec((None, 8, d), lambda b, i: (b, jnp.maximum(i * (tm // 8) - 1, 0), 0))

    def full(arr):
        nd = arr.ndim
        return pl.BlockSpec(arr.shape, lambda b, i: (0,) * nd, pipeline_mode=pl.Buffered(1))

    weights = [p["mu"], p["w_r"], p["w_k"], p["w_v"], p["w0"], p["w1"], p["w2"], p["a0"], p["a1"], p["a2"],
               p["g1"], p["g2"], p["k_k"], p["k_a"], p["r_k"]]
    out_bf = jax.ShapeDtypeStruct((bsz, t, d), BF16)
    out_shape = [out_bf] * 7 + [jax.ShapeDtypeStruct((bsz, t // WKV_L, d), F32)]
    out_specs = [row_spec] * 7 + [pl.BlockSpec((None, tm // WKV_L, d), lambda b, i: (b, i, 0))]
    return pl.pallas_call(
        _rwkv_proj_kernel,
        grid=(bsz, nt),
        in_specs=[row_spec, prev_spec] + [full(w) for w in weights],
        out_specs=out_specs,
        out_shape=out_shape,
        compiler_params=pltpu.CompilerParams(dimension_semantics=("parallel", "parallel"),
                                             vmem_limit_bytes=VMEM_LIMIT),
        name="rwkv_proj",
    )(x, x, *weights)


def _wkv_kernel(rt_ref, kt_ref, bt_ref, at_ref, v_ref, gl_ref, y_ref, st_ref, t_scr, rb_scr, x1_scr):
    tb, d = rt_ref.shape
    L, W = WKV_L, MXU_DIM
    npack = d // W
    row = _iota((L, W), 0)
    sidx = _iota((L, W), 1) % L
    strict = sidx < row
    incl = sidx <= row
    eye = sidx == row
    r2 = _iota((W, W), 0)
    c2 = _iota((W, W), 1)
    bd = (r2 // L) == (c2 // L)
    diag = r2 == c2

    def stack(xb):
        return jnp.where(bd, jnp.concatenate([xb] * PACK, axis=0), jnp.zeros((), xb.dtype))

    @pl.when(pl.program_id(1) == 0)
    def _():
        st_ref[...] = jnp.zeros_like(st_ref)

    def rows(c):
        return pl.ds(pl.multiple_of(c * L, L), L)

    def prepare(c, pk, slot):
        sl = rows(c)
        ln = slice(pk * W, (pk + 1) * W)
        rt = rt_ref[sl, ln]
        kt = kt_ref[sl, ln]
        bt = bt_ref[sl, ln]
        at = at_ref[sl, ln]
        v = v_ref[sl, ln]
        gmat = _dot_nt(jnp.concatenate([at, rt], axis=0),
                       jnp.concatenate([stack(bt), stack(kt)], axis=0))
        yield
        ab = gmat[:L, :W]
        ak = gmat[:L, W:]
        rb = gmat[L:, :W]
        rk = gmat[L:, W:]
        n1 = jnp.where(strict, ab, 0.0)
        tm_ = jnp.where(eye, 1.0, 0.0) + n1
        cur = n1.astype(BF16)
        cur = _dot(cur, stack(cur)).astype(BF16)
        res = _dot(jnp.concatenate([jnp.where(strict, ak, 0.0), jnp.where(incl, rk, 0.0)], axis=0), stack(v))
        yield
        for _ in range(4):
            nres = _dot(jnp.concatenate([cur, tm_.astype(BF16)], axis=0), stack(cur))
            tm_ = tm_ + nres[L:]
            cur = nres[:L].astype(BF16)
            yield
        tm_ = tm_ + _dot(tm_, stack(cur))
        yield
        t_scr[slot, pk] = tm_.astype(BF16)
        rb_scr[slot, pk] = jnp.where(incl, rb, 0.0).astype(BF16)
        x1_scr[slot, pk] = res

    def advance(c, pk, slot):
        sl = rows(c)
        ln = slice(pk * W, (pk + 1) * W)
        tm_ = t_scr[slot, pk]
        rbi = rb_scr[slot, pk]
        x1r = x1_scr[slot, pk]
        rt = rt_ref[sl, ln]
        kt = kt_ref[sl, ln]
        bt = bt_ref[sl, ln]
        at = at_ref[sl, ln]
        v = v_ref[sl, ln]
        gl = gl_ref[pl.ds(c, 1), ln]
        st = st_ref[pk]
        ra = _dot(jnp.concatenate([rt, at], axis=0), st)
        yield
        ub = _dot(tm_, stack((x1r[:L] + ra[L:]).astype(BF16))).astype(BF16)
        yield
        yr = _dot(rbi, stack(ub))
        bh = (bt.astype(F32) * gl).astype(BF16)
        kh = (kt.astype(F32) * gl).astype(BF16)
        upd = _dot_tn(jnp.concatenate([bh, kh], axis=0), jnp.concatenate([ub, v], axis=0))
        yield
        glcol = jnp.sum(jnp.where(diag, gl, 0.0), axis=1, keepdims=True)
        st_ref[pk] = glcol * st + jnp.where(bd, upd, 0.0)
        y_ref[sl, ln] = (ra[:L] + yr + x1r[L:]).astype(y_ref.dtype)

    def run(*gens):
        for _ in itertools.zip_longest(*gens):
            pass

    npair = tb // (2 * L)
    packs = range(npack)

    def advance2(c, s0, s1, pk):
        return itertools.chain(advance(c, pk, s0), advance(c + 1, pk, s1))

    run(*([prepare(0, pk, 0) for pk in packs] + [prepare(1, pk, 1) for pk in packs]))

    def body(i, carry):
        c = 2 * i
        s0 = c % 4
        run(*([advance2(c, s0, s0 + 1, pk) for pk in packs]
              + [prepare(c + 2, pk, (s0 + 2) % 4) for pk in packs]
              + [prepare(c + 3, pk, (s0 + 3) % 4) for pk in packs]))
        return carry

    lax.fori_loop(0, npair - 1, body, 0)
    c_last = 2 * (npair - 1)
    run(*[advance2(c_last, c_last % 4, c_last % 4 + 1, pk) for pk in packs])


def _wkv(rt, kt, bt, at, v, gl, tb=WKV_TIME_BLOCK):
    bsz, t, d = rt.shape
    npack = d // MXU_DIM
    seq_spec = pl.BlockSpec((None, tb, d), lambda b, i: (b, i, 0))
    gl_spec = pl.BlockSpec((None, tb // WKV_L, d), lambda b, i: (b, i, 0))
    return pl.pallas_call(
        _wkv_kernel,
        grid=(bsz, t // tb),
        in_specs=[seq_spec] * 5 + [gl_spec],
        out_specs=seq_spec,
        out_shape=jax.ShapeDtypeStruct((bsz, t, d), BF16),
        scratch_shapes=[pltpu.VMEM((npack, MXU_DIM, MXU_DIM), F32),
                        pltpu.VMEM((4, npack, WKV_L, MXU_DIM), BF16),
                        pltpu.VMEM((4, npack, WKV_L, MXU_DIM), BF16),
                        pltpu.VMEM((4, npack, 2 * WKV_L, MXU_DIM), F32)],
        compiler_params=pltpu.CompilerParams(dimension_semantics=("parallel", "arbitrary"),
                                             vmem_limit_bytes=VMEM_LIMIT),
        name="wkv7",
    )(rt, kt, bt, at, v, gl)


def _post_block_kernel(*refs, rwkv, project, sub_rows):
    it = iter(refs)
    x_ref = next(it)
    y_ref = next(it)
    if rwkv:
        bonus_ref, g_ref, lnxg_ref, lnxb_ref = next(it), next(it), next(it), next(it)
    wo_ref, ln_ref, w1_ref, w2_ref = next(it), next(it), next(it), next(it)
    if project:
        wq_ref, wkv_ref = next(it), next(it)
    out_ref = next(it)
    if project:
        q_ref, k_ref, v_ref = next(it), next(it), next(it)

    tm, d = x_ref.shape
    ln = ln_ref[...]
    ones_bd = _head_block_ones() if rwkv else None

    def sub_tile(rs):
        x = x_ref[rs, :]
        if rwkv:
            y = y_ref[rs, :].astype(F32)
            mean = _head_sum_bcast(y, ones_bd) * (1.0 / HEAD)
            yc = y - mean
            var = _head_sum_bcast(yc * yc, ones_bd) * (1.0 / HEAD)
            yn = yc * lax.rsqrt(var + GN_EPS) * lnxg_ref[...] + lnxb_ref[...]
            mixed = ((yn + bonus_ref[rs, :].astype(F32)) * g_ref[rs, :].astype(F32)).astype(BF16)
        else:
            mixed = y_ref[rs, :]
        h = _dot(mixed, wo_ref[...])
        yield
        x1 = _layer_norm(ALPHA * x + h, ln[0:1, :], ln[1:2, :], LN_EPS)
        yield
        hid = jnp.maximum(_dot(x1, w1_ref[...]), 0.0)
        h2 = _dot(hid * hid, w2_ref[...])
        yield
        x2 = _layer_norm(ALPHA * x1 + h2, ln[2:3, :], ln[3:4, :], LN_EPS)
        out_ref[rs, :] = x2
        yield
        if project:
            x2b = x2.astype(BF16)
            q_ref[rs, :] = (_dot(x2b, wq_ref[...]) * (HEAD ** -0.5 * LOG2E)).astype(BF16)
            kv = _dot(x2b, wkv_ref[...])
            k_ref[rs, :] = kv[:, :d].astype(BF16)
            v_ref[rs, :] = kv[:, d:].astype(BF16)

    _run_staggered([sub_tile(slice(r, r + sub_rows)) for r in range(0, tm, sub_rows)])


def _post_block(x, y, rw, w_o, ln, w1, w2, layer, proj, sub_rows=POST_SUB_ROWS):
    bsz, t, d = x.shape
    rwkv = rw is not None
    project = proj is not None
    tm = 2 * sub_rows
    row_spec = pl.BlockSpec((None, tm, d), lambda b, i: (b, i, 0))

    def full(arr):
        nd = arr.ndim
        return pl.BlockSpec(arr.shape, lambda b, i: (0,) * nd, pipeline_mode=pl.Buffered(1))

    def slab(arr):
        return pl.BlockSpec((None,) + arr.shape[1:], lambda b, i: (layer, 0, 0), pipeline_mode=pl.Buffered(1))

    args = [x, y]
    specs = [row_spec, row_spec]
    if rwkv:
        args += [rw["bonus"], rw["g"], rw["lnx_g"], rw["lnx_b"]]
        specs += [row_spec, row_spec, full(rw["lnx_g"]), full(rw["lnx_b"])]
    args += [w_o, ln, w1, w2]
    specs += [full(w_o), full(ln), slab(w1), slab(w2)]
    out_shape = [jax.ShapeDtypeStruct((bsz, t, d), F32)]
    out_specs = [row_spec]
    if project:
        args += [proj["w_q"], proj["w_kv"]]
        specs += [full(proj["w_q"]), full(proj["w_kv"])]
        out_shape += [jax.ShapeDtypeStruct((bsz, t, d), BF16)] * 3
        out_specs += [row_spec] * 3
    return pl.pallas_call(
        functools.partial(_post_block_kernel, rwkv=rwkv, project=project, sub_rows=sub_rows),
        grid=(bsz, t // tm),
        in_specs=specs,
        out_specs=out_specs,
        out_shape=out_shape,
        compiler_params=pltpu.CompilerParams(dimension_semantics=("parallel", "parallel"),
                                             vmem_limit_bytes=VMEM_LIMIT),
        name="post_rwkv" if rwkv else "post_attn",
    )(*args)


def _t5_bucket_np(rel):
    nb = REL_BUCKETS // 2
    max_exact = nb // 2
    n = np.abs(rel)
    thresholds = [int(math.ceil(max_exact * (REL_MAX_DIST / max_exact) ** (m / (nb - max_exact)) - 1e-9))
                  for m in range(1, nb - max_exact)]
    large = max_exact + sum((n >= th).astype(np.int64) for th in thresholds)
    return np.where(rel > 0, nb, 0) + np.where(n < max_exact, n, np.minimum(large, nb - 1))


def _bias_bucket_tables():
    kl = np.arange(KEY_TILE)[:, None]
    qpos = np.arange(2 * Q_BLOCK)[None, :]
    out = np.zeros((2, KEY_TILE, 2 * Q_BLOCK), np.int32)
    out[0] = _t5_bucket_np(kl - KEY_TILE - qpos)
    allowed = (kl // ATT_CHUNK) <= (qpos // ATT_CHUNK)
    out[1] = np.where(allowed, _t5_bucket_np(kl - qpos), -1)
    return out


def _bias_table_kernel(bucket_ref, relb_ref, out_ref):
    h = pl.program_id(0)
    bucket = bucket_ref[...]
    acc = jnp.where(bucket < 0, NEG_BIG, 0.0).astype(F32)
    for b in range(REL_BUCKETS):
        acc = jnp.where(bucket == b, relb_ref[b, h], acc)
    out_ref[...] = jnp.where(bucket < 0, NEG_BIG, acc * LOG2E)


def _bias_tables(rel_bias):
    nh = rel_bias.shape[1]
    buckets = jnp.asarray(_bias_bucket_tables())
    return pl.pallas_call(
        _bias_table_kernel,
        grid=(nh,),
        in_specs=[pl.BlockSpec(buckets.shape, lambda h: (0, 0, 0)),
                  pl.BlockSpec(memory_space=pltpu.SMEM)],
        out_specs=pl.BlockSpec((None,) + buckets.shape, lambda h: (h, 0, 0, 0)),
        out_shape=jax.ShapeDtypeStruct((nh,) + buckets.shape, F32),
        name="t5_bias_tables",
    )(buckets, rel_bias)


def _diff_attn_kernel(q_ref, k_ref, v_ref, tab_ref, relb_ref, lam_ref, sg_ref, o_ref, s_scr, p_scr, ve_scr, *,
                      lambda_init):
    t = q_ref.shape[0]
    qb, kt_, dh = Q_BLOCK, KEY_TILE, HEAD
    dv = 2 * dh
    nhead = q_ref.shape[1] // dv
    h0 = pl.program_id(1) * nhead
    lam = lam_ref[...]
    lam_full = (jnp.exp(jnp.sum(lam[0:1, :] * lam[1:2, :], axis=-1, keepdims=True))
                - jnp.exp(jnp.sum(lam[2:3, :] * lam[3:4, :], axis=-1, keepdims=True)) + lambda_init)
    qp = 2 * qb
    lane = _iota((qp, dv), 1)
    sg = sg_ref[...] * (1.0 - lambda_init)
    for hh in range(nhead):
        ve_scr[hh, :dv, :] = jnp.transpose(v_ref[:, hh * dv:(hh + 1) * dv].astype(F32)).astype(BF16)
        ve_scr[hh, dv:, :] = jnp.ones((ve_scr.shape[1] - dv, t), BF16)

    npairs = t // kt_
    col_max = {}
    dyn0 = jnp.minimum(pl.program_id(0), 0)

    def logits(idx, hh, jl):
        slot = idx % 2 + dyn0
        hs = slice(hh * dv, (hh + 1) * dv)
        c_far = relb_ref[REL_BUCKETS // 2 - 1, h0 + hh] * LOG2E
        q = q_ref[jl * qp:(jl + 1) * qp, hs]
        zero = jnp.zeros((), q.dtype)
        qm = jnp.concatenate([jnp.where(lane < dh, q, zero), jnp.where(lane >= dh, q, zero)], axis=0)
        mx_far = jnp.full((1, 2 * qp), NEG_BIG, F32)
        mx_near = jnp.full((1, 2 * qp), NEG_BIG, F32)
        for j in range(jl + 1):
            ks = slice(j * kt_, (j + 1) * kt_)
            s = _dot_nt(k_ref[ks, hs], qm)
            if j >= jl - 1:
                tab = tab_ref[hh, 1 if j == jl else 0]
                s = s + jnp.concatenate([tab, tab], axis=1)
                mx_near = jnp.maximum(mx_near, jnp.max(s, axis=0, keepdims=True))
            else:
                mx_far = jnp.maximum(mx_far, jnp.max(s, axis=0, keepdims=True))
            s_scr[slot, ks, :] = s
            yield
        col_max[idx] = (jnp.maximum(mx_near, mx_far + c_far), c_far)

    def outputs(idx, hh, jl):
        slot = idx % 2 + dyn0
        m, c_far = col_max.pop(idx)
        m_far = m - c_far
        for j in range(jl + 1):
            ks = slice(j * kt_, (j + 1) * kt_)
            p_scr[ks, :] = jnp.exp2(s_scr[slot, ks, :] - (m if j >= jl - 1 else m_far)).astype(BF16)
            yield
        kend = (jl + 1) * kt_
        acc = jnp.dot(ve_scr[hh, :, :kend], p_scr[:kend, :], preferred_element_type=F32)
        pv = acc[:dv, :] / acc[dv:dv + 1, :]
        o = pv[:, :qp] - lam_full * pv[:, qp:]
        o = o * lax.rsqrt(jnp.mean(o * o, axis=0, keepdims=True) + SUBLN_EPS)
        o_ref[jl * qp:(jl + 1) * qp, hh * dv:(hh + 1) * dv] = jnp.transpose(o * sg).astype(o_ref.dtype)

    def run(*gens):
        for _ in itertools.zip_longest(*gens):
            pass

    items = [(hh, jl) for hh in range(nhead) for jl in reversed(range(npairs))]
    run(logits(0, *items[0]))
    for idx, item in enumerate(items):
        nxt = [logits(idx + 1, *items[idx + 1])] if idx + 1 < len(items) else []
        run(outputs(idx, *item), *nxt)


def _diff_attn(q, k, v, tabs, rel_bias, lam, subln_g, lambda_init, heads_per_step=ATTN_HEADS_PER_STEP):
    bsz, t, d = q.shape
    nh = d // (2 * HEAD)
    hps = heads_per_step
    seq_spec = pl.BlockSpec((None, t, hps * 2 * HEAD), lambda b, h: (b, 0, h))
    return pl.pallas_call(
        functools.partial(_diff_attn_kernel, lambda_init=lambda_init),
        grid=(bsz, nh // hps),
        in_specs=[seq_spec, seq_spec, seq_spec,
                  pl.BlockSpec((hps,) + tabs.shape[1:], lambda b, h: (h, 0, 0, 0)),
                  pl.BlockSpec(memory_space=pltpu.SMEM),
                  pl.BlockSpec(lam.shape, lambda b, h: (0, 0)),
                  pl.BlockSpec(subln_g.shape, lambda b, h: (0, 0))],
        out_specs=seq_spec,
        out_shape=jax.ShapeDtypeStruct((bsz, t, d), BF16),
        scratch_shapes=[pltpu.VMEM((2, t, 4 * Q_BLOCK), F32),
                        pltpu.VMEM((t, 4 * Q_BLOCK), BF16),
                        pltpu.VMEM((hps, 2 * HEAD + 16, t), BF16)],
        compiler_params=pltpu.CompilerParams(dimension_semantics=("parallel", "parallel"),
                                             vmem_limit_bytes=VMEM_LIMIT),
        name="diff_attn",
    )(q, k, v, tabs, rel_bias, lam, subln_g)


def kernel(x, a_mu, a_w_r, a_w_k, a_w_v, a_w_o, a_w0, a_w1, a_w2, a_a0, a_a1, a_a2, a_g1, a_g2, a_k_k, a_k_a,
           a_r_k, a_lnx_g, a_lnx_b, b_w_kv, b_w_q, b_lam, b_subln_g, b_w_o, rel_bias, mlp_w1, mlp_w2, ln_g, ln_b):
    bsz, t, d = x.shape
    assert d % MXU_DIM == 0 and t % WKV_TIME_BLOCK == 0 and t % PROJ_ROWS == 0 and t % (2 * POST_SUB_ROWS) == 0
    bf = lambda w: w.astype(BF16)
    row = lambda w: w.reshape(1, d)

    p = dict(mu=a_mu[0], w_r=bf(a_w_r[0]), w_k=bf(a_w_k[0]), w_v=bf(a_w_v[0]), w0=row(a_w0[0]),
             w1=bf(a_w1[0]), w2=bf(a_w2[0]), a0=row(a_a0[0]), a1=bf(a_a1[0]), a2=bf(a_a2[0]),
             g1=bf(a_g1[0]), g2=bf(a_g2[0]), k_k=row(a_k_k[0]), k_a=row(a_k_a[0]), r_k=row(a_r_k[0]))
    rt, kt, bt, at, v, bonus, g, gl = _rwkv_proj(x, p)
    y = _wkv(rt, kt, bt, at, v, gl)
    ln0 = jnp.stack([ln_g[0, 0], ln_b[0, 0], ln_g[0, 1], ln_b[0, 1]])
    w1_all, w2_all = bf(mlp_w1), bf(mlp_w2)
    x, q, k_sh, v_sh = _post_block(
        x, y, dict(bonus=bonus, g=g, lnx_g=row(a_lnx_g[0]), lnx_b=row(a_lnx_b[0])),
        bf(a_w_o[0]), ln0, w1_all, w2_all, 0, dict(w_q=bf(b_w_q[0]), w_kv=bf(b_w_kv)))

    lambda_init = 0.8 - 0.6 * math.exp(-0.3 * 1)
    tabs = _bias_tables(rel_bias)
    o = _diff_attn(q, k_sh, v_sh, tabs, rel_bias, b_lam[0], b_subln_g[0].reshape(2 * HEAD, 1), lambda_init)
    ln1 = jnp.stack([ln_g[1, 0], ln_b[1, 0], ln_g[1, 1], ln_b[1, 1]])
    (x,) = _post_block(x, o, None, bf(b_w_o[0]), ln1, w1_all, w2_all, 1, None)
    return x
```

```python
import functools
import itertools
import math

import numpy as np
import jax
import jax.numpy as jnp
from jax import lax
from jax.experimental import pallas as pl
from jax.experimental.pallas import tpu as pltpu

F32 = jnp.float32
BF16 = jnp.bfloat16

DEPTH = 2
ALPHA = (2.0 * DEPTH) ** 0.25
HEAD = 64
GN_EPS = 64e-5
SUBLN_EPS = 1e-5
LN_EPS = 1e-5
REL_BUCKETS = 32
REL_MAX_DIST = 128
ATT_CHUNK = 64
Q_BLOCK = 128

LANES = 128
MXU_DIM = 256
WKV_L = 64
PACK = MXU_DIM // HEAD
KEY_TILE = 256

PROJ_ROWS = 512
POST_SUB_ROWS = 256
WKV_TIME_BLOCK = 1024
ATTN_HEADS_PER_STEP = 2
NEG_BIG = -1e30
LOG2E = math.log2(math.e)
VMEM_LIMIT = 56 * 1024 * 1024


def _dot(a, b):
    return jnp.dot(a.astype(BF16), b.astype(BF16), preferred_element_type=F32)


def _dot_nt(a, b):
    return lax.dot_general(a.astype(BF16), b.astype(BF16), (((1,), (1,)), ((), ())),
                           preferred_element_type=F32)


def _dot_tn(a, b):
    return lax.dot_general(a.astype(BF16), b.astype(BF16), (((0,), (0,)), ((), ())),
                           preferred_element_type=F32)


def _iota(shape, dim):
    return lax.broadcasted_iota(jnp.int32, shape, dim)


def _head_block_ones():
    r = _iota((MXU_DIM, MXU_DIM), 0) // HEAD
    c = _iota((MXU_DIM, MXU_DIM), 1) // HEAD
    return jnp.where(r == c, 1.0, 0.0).astype(BF16)


def _head_sum_bcast(x, ones_bd):
    d = x.shape[1]
    parts = [_dot(x[:, p:p + MXU_DIM], ones_bd) for p in range(0, d, MXU_DIM)]
    return jnp.concatenate(parts, axis=1)


def _run_staggered(gens):
    gens = list(gens)
    live = []
    while gens or live:
        if gens:
            live.append(gens.pop(0))
        for g in list(live):
            if next(g, StopIteration) is StopIteration:
                live.remove(g)


def _layer_norm(x, g, b, eps):
    mu = jnp.mean(x, axis=-1, keepdims=True)
    xc = x - mu
    var = jnp.mean(xc * xc, axis=-1, keepdims=True)
    return xc * lax.rsqrt(var + eps) * g + b


def _rwkv_proj_kernel(x_ref, xp_ref, mu_ref, wr_ref, wk_ref, wv_ref, w0_ref, w1_ref, w2_ref,
                      a0_ref, a1_ref, a2_ref, g1_ref, g2_ref, kk_ref, ka_ref, rk_ref,
                      rt_ref, kt_ref, bt_ref, at_ref, v_ref, bonus_ref, g_ref, gl_ref):
    i = pl.program_id(1)
    x = x_ref[...]
    tm, d = x.shape
    prev_row = jnp.where(i == 0, 0.0, xp_ref[7:8, :])
    row = _iota((tm, d), 0)
    xprev = jnp.where(row == 0, prev_row, pltpu.roll(x, 1, 0))
    xx = xprev - x
    mu = mu_ref[...]

    xr, xw, xk, xv, xa, xg = [(x + xx * mu[j:j + 1, :]).astype(BF16) for j in range(6)]
    hw = jnp.tanh(_dot(xw, w1_ref[...])).astype(BF16)
    ha = _dot(xa, a1_ref[...]).astype(BF16)
    hg = jax.nn.sigmoid(_dot(xg, g1_ref[...])).astype(BF16)

    ones_bd = _head_block_ones()
    tr = _iota((MXU_DIM, MXU_DIM), 0)
    tc = _iota((MXU_DIM, MXU_DIM), 1)
    tri = jnp.where((tr // WKV_L == tc // WKV_L) & (tc <= tr), 1.0, 0.0).astype(BF16)

    def col_group(c0):
        cl = slice(c0, c0 + MXU_DIM)
        r = _dot(xr, wr_ref[:, cl])
        k = _dot(xk, wk_ref[:, cl])
        v = _dot(xv, wv_ref[:, cl])
        wraw = w0_ref[:, cl] + _dot(hw, w2_ref[:, cl])
        a = jax.nn.sigmoid(a0_ref[:, cl] + _dot(ha, a2_ref[:, cl]))
        g = _dot(hg, g2_ref[:, cl])
        yield
        ld = -math.exp(-0.5) * jax.nn.sigmoid(wraw)
        kk = k * kk_ref[:, cl]
        kk = kk * jnp.minimum(lax.rsqrt(_dot(kk * kk, ones_bd)), 1e12)
        k = k * (1.0 + (a - 1.0) * ka_ref[:, cl])
        bonus = _dot(r * k * rk_ref[:, cl], ones_bd) * v
        ld_hi = ld.astype(BF16)
        ld_lo = (ld - ld_hi.astype(F32)).astype(BF16)
        cs = jnp.concatenate(
            [jnp.dot(tri, ld_hi[p:p + MXU_DIM, :], preferred_element_type=F32)
             + jnp.dot(tri, ld_lo[p:p + MXU_DIM, :], preferred_element_type=F32)
             for p in range(0, tm, MXU_DIM)], axis=0)
        yield
        igam = jnp.exp(-cs)
        rt_ref[:, cl] = (r * jnp.exp(cs)).astype(BF16)
        kt_ref[:, cl] = (k * igam).astype(BF16)
        bt_ref[:, cl] = (kk * a * igam).astype(BF16)
        at_ref[:, cl] = (-kk * jnp.exp(cs - ld)).astype(BF16)
        v_ref[:, cl] = v.astype(BF16)
        bonus_ref[:, cl] = bonus.astype(BF16)
        g_ref[:, cl] = g.astype(BF16)
        gl_ref[:, cl] = jnp.exp(cs.reshape(tm // WKV_L, WKV_L, MXU_DIM)[:, WKV_L - 1, :])

    _run_staggered([col_group(c0) for c0 in range(0, d, MXU_DIM)])


def _rwkv_proj(x, p, tm=PROJ_ROWS):
    bsz, t, d = x.shape
    nt = t // tm
    row_spec = pl.BlockSpec((None, tm, d), lambda b, i: (b, i, 0))
    prev_spec = pl.BlockSpec((None, 8, d), lambda b, i: (b, jnp.maximum(i * (tm // 8) - 1, 0), 0))

    def full(arr):
        nd = arr.ndim
        return pl.BlockSpec(arr.shape, lambda b, i: (0,) * nd, pipeline_mode=pl.Buffered(1))

    weights = [p["mu"], p["w_r"], p["w_k"], p["w_v"], p["w0"], p["w1"], p["w2"], p["a0"], p["a1"], p["a2"],
               p["g1"], p["g2"], p["k_k"], p["k_a"], p["r_k"]]
    out_bf = jax.ShapeDtypeStruct((bsz, t, d), BF16)
    out_shape = [out_bf] * 7 + [jax.ShapeDtypeStruct((bsz, t // WKV_L, d), F32)]
    out_specs = [row_spec] * 7 + [pl.BlockSpec((None, tm // WKV_L, d), lambda b, i: (b, i, 0))]
    return pl.pallas_call(
        _rwkv_proj_kernel,
        grid=(bsz, nt),
        in_specs=[row_spec, prev_spec] + [full(w) for w in weights],
        out_specs=out_specs,
        out_shape=out_shape,
        compiler_params=pltpu.CompilerParams(dimension_semantics=("parallel", "parallel"),
                                             vmem_limit_bytes=VMEM_LIMIT),
        name="rwkv_proj",
    )(x, x, *weights)


def _wkv_kernel(rt_ref, kt_ref, bt_ref, at_ref, v_ref, gl_ref, y_ref, st_ref, t_scr, rb_scr, x1_scr):
    tb, d = rt_ref.shape
    L, W = WKV_L, MXU_DIM
    npack = d // W
    row = _iota((L, W), 0)
    sidx = _iota((L, W), 1) % L
    strict = sidx < row
    incl = sidx <= row
    eye = sidx == row
    r2 = _iota((W, W), 0)
    c2 = _iota((W, W), 1)
    bd = (r2 // L) == (c2 // L)
    diag = r2 == c2

    def stack(xb):
        return jnp.where(bd, jnp.concatenate([xb] * PACK, axis=0), jnp.zeros((), xb.dtype))

    @pl.when(pl.program_id(1) == 0)
    def _():
        st_ref[...] = jnp.zeros_like(st_ref)

    dyn0 = jnp.minimum(pl.program_id(0), 0)

    def rows(c):
        return pl.ds(pl.multiple_of(c * L, L), L)

    def prepare(c, pk, slot):
        slot = slot + dyn0
        sl = rows(c)
        ln = slice(pk * W, (pk + 1) * W)
        rt = rt_ref[sl, ln]
        kt = kt_ref[sl, ln]
        bt = bt_ref[sl, ln]
        at = at_ref[sl, ln]
        v = v_ref[sl, ln]
        gmat = _dot_nt(jnp.concatenate([at, rt], axis=0),
                       jnp.concatenate([stack(bt), stack(kt)], axis=0))
        yield
        ab = gmat[:L, :W]
        ak = gmat[:L, W:]
        rb = gmat[L:, :W]
        rk = gmat[L:, W:]
        n1 = jnp.where(strict, ab, 0.0)
        tm_ = jnp.where(eye, 1.0, 0.0) + n1
        cur = n1.astype(BF16)
        cur = _dot(cur, stack(cur)).astype(BF16)
        res = _dot(jnp.concatenate([jnp.where(strict, ak, 0.0), jnp.where(incl, rk, 0.0)], axis=0), stack(v))
        yield
        for _ in range(4):
            nres = _dot(jnp.concatenate([cur, tm_.astype(BF16)], axis=0), stack(cur))
            tm_ = tm_ + nres[L:]
            cur = nres[:L].astype(BF16)
            yield
        tm_ = tm_ + _dot(tm_, stack(cur))
        yield
        t_scr[slot, pk] = tm_.astype(BF16)
        rb_scr[slot, pk] = jnp.where(incl, rb, 0.0).astype(BF16)
        x1_scr[slot, pk] = res

    def advance(c, pk, slot):
        slot = slot + dyn0
        sl = rows(c)
        ln = slice(pk * W, (pk + 1) * W)
        tm_ = t_scr[slot, pk]
        rbi = rb_scr[slot, pk]
        x1r = x1_scr[slot, pk]
        rt = rt_ref[sl, ln]
        kt = kt_ref[sl, ln]
        bt = bt_ref[sl, ln]
        at = at_ref[sl, ln]
        v = v_ref[sl, ln]
        gl = gl_ref[pl.ds(c, 1), ln]
        st = st_ref[pk]
        ra = _dot(jnp.concatenate([rt, at], axis=0), st)
        yield
        ub = _dot(tm_, stack((x1r[:L] + ra[L:]).astype(BF16))).astype(BF16)
        yield
        yr = _dot(rbi, stack(ub))
        bh = (bt.astype(F32) * gl).astype(BF16)
        kh = (kt.astype(F32) * gl).astype(BF16)
        upd = _dot_tn(jnp.concatenate([bh, kh], axis=0), jnp.concatenate([ub, v], axis=0))
        yield
        glcol = jnp.sum(jnp.where(diag, gl, 0.0), axis=1, keepdims=True)
        st_ref[pk] = glcol * st + jnp.where(bd, upd, 0.0)
        y_ref[sl, ln] = (ra[:L] + yr + x1r[L:]).astype(y_ref.dtype)

    def run(*gens):
        for _ in itertools.zip_longest(*gens):
            pass

    npair = tb // (2 * L)
    packs = range(npack)

    def advance2(c, s0, s1, pk):
        return itertools.chain(advance(c, pk, s0), advance(c + 1, pk, s1))

    run(*([prepare(0, pk, 0) for pk in packs] + [prepare(1, pk, 1) for pk in packs]))

    def body(i, carry):
        c = 2 * i
        s0 = c % 4
        run(*([advance2(c, s0, s0 + 1, pk) for pk in packs]
              + [prepare(c + 2, pk, (s0 + 2) % 4) for pk in packs]
              + [prepare(c + 3, pk, (s0 + 3) % 4) for pk in packs]))
        return carry

    lax.fori_loop(0, npair - 1, body, 0, unroll=True)
    c_last = 2 * (npair - 1)
    run(*[advance2(c_last, c_last % 4, c_last % 4 + 1, pk) for pk in packs])


def _wkv(rt, kt, bt, at, v, gl, tb=WKV_TIME_BLOCK):
    bsz, t, d = rt.shape
    npack = d // MXU_DIM
    seq_spec = pl.BlockSpec((None, tb, d), lambda b, i: (b, i, 0))
    gl_spec = pl.BlockSpec((None, tb // WKV_L, d), lambda b, i: (b, i, 0))
    return pl.pallas_call(
        _wkv_kernel,
        grid=(bsz, t // tb),
        in_specs=[seq_spec] * 5 + [gl_spec],
        out_specs=seq_spec,
        out_shape=jax.ShapeDtypeStruct((bsz, t, d), BF16),
        scratch_shapes=[pltpu.VMEM((npack, MXU_DIM, MXU_DIM), F32),
                        pltpu.VMEM((4, npack, WKV_L, MXU_DIM), BF16),
                        pltpu.VMEM((4, npack, WKV_L, MXU_DIM), BF16),
                        pltpu.VMEM((4, npack, 2 * WKV_L, MXU_DIM), F32)],
        compiler_params=pltpu.CompilerParams(dimension_semantics=("parallel", "arbitrary"),
                                             vmem_limit_bytes=VMEM_LIMIT),
        name="wkv7",
    )(rt, kt, bt, at, v, gl)


def _post_block_kernel(*refs, rwkv, project, sub_rows):
    it = iter(refs)
    x_ref = next(it)
    y_ref = next(it)
    if rwkv:
        bonus_ref, g_ref, lnxg_ref, lnxb_ref = next(it), next(it), next(it), next(it)
    wo_ref, ln_ref, w1_ref, w2_ref = next(it), next(it), next(it), next(it)
    if project:
        wq_ref, wkv_ref = next(it), next(it)
    out_ref = next(it)
    if project:
        q_ref, k_ref, v_ref = next(it), next(it), next(it)

    tm, d = x_ref.shape
    ln = ln_ref[...]
    ones_bd = _head_block_ones() if rwkv else None

    def sub_tile(rs):
        x = x_ref[rs, :]
        if rwkv:
            y = y_ref[rs, :].astype(F32)
            mean = _head_sum_bcast(y, ones_bd) * (1.0 / HEAD)
            yc = y - mean
            var = _head_sum_bcast(yc * yc, ones_bd) * (1.0 / HEAD)
            yn = yc * lax.rsqrt(var + GN_EPS) * lnxg_ref[...] + lnxb_ref[...]
            mixed = ((yn + bonus_ref[rs, :].astype(F32)) * g_ref[rs, :].astype(F32)).astype(BF16)
        else:
            mixed = y_ref[rs, :]
        h = _dot(mixed, wo_ref[...])
        yield
        x1 = _layer_norm(ALPHA * x + h, ln[0:1, :], ln[1:2, :], LN_EPS)
        yield
        hid = jnp.maximum(_dot(x1, w1_ref[...]), 0.0)
        h2 = _dot(hid * hid, w2_ref[...])
        yield
        x2 = _layer_norm(ALPHA * x1 + h2, ln[2:3, :], ln[3:4, :], LN_EPS)
        out_ref[rs, :] = x2
        yield
        if project:
            x2b = x2.astype(BF16)
            q_ref[rs, :] = (_dot(x2b, wq_ref[...]) * (HEAD ** -0.5 * LOG2E)).astype(BF16)
            kv = _dot(x2b, wkv_ref[...])
            k_ref[rs, :] = kv[:, :d].astype(BF16)
            v_ref[rs, :] = kv[:, d:].astype(BF16)

    _run_staggered([sub_tile(slice(r, r + sub_rows)) for r in range(0, tm, sub_rows)])


def _post_block(x, y, rw, w_o, ln, w1, w2, layer, proj, sub_rows=POST_SUB_ROWS):
    bsz, t, d = x.shape
    rwkv = rw is not None
    project = proj is not None
    tm = 2 * sub_rows
    row_spec = pl.BlockSpec((None, tm, d), lambda b, i: (b, i, 0))

    def full(arr):
        nd = arr.ndim
        return pl.BlockSpec(arr.shape, lambda b, i: (0,) * nd, pipeline_mode=pl.Buffered(1))

    def slab(arr):
        return pl.BlockSpec((None,) + arr.shape[1:], lambda b, i: (layer, 0, 0), pipeline_mode=pl.Buffered(1))

    args = [x, y]
    specs = [row_spec, row_spec]
    if rwkv:
        args += [rw["bonus"], rw["g"], rw["lnx_g"], rw["lnx_b"]]
        specs += [row_spec, row_spec, full(rw["lnx_g"]), full(rw["lnx_b"])]
    args += [w_o, ln, w1, w2]
    specs += [full(w_o), full(ln), slab(w1), slab(w2)]
    out_shape = [jax.ShapeDtypeStruct((bsz, t, d), F32)]
    out_specs = [row_spec]
    if project:
        args += [proj["w_q"], proj["w_kv"]]
        specs += [full(proj["w_q"]), full(proj["w_kv"])]
        out_shape += [jax.ShapeDtypeStruct((bsz, t, d), BF16)] * 3
        out_specs += [row_spec] * 3
    return pl.pallas_call(
        functools.partial(_post_block_kernel, rwkv=rwkv, project=project, sub_rows=sub_rows),
        grid=(bsz, t // tm),
        in_specs=specs,
        out_specs=out_specs,
        out_shape=out_shape,
        compiler_params=pltpu.CompilerParams(dimension_semantics=("parallel", "parallel"),
                                             vmem_limit_bytes=VMEM_LIMIT),
        name="post_rwkv" if rwkv else "post_attn",
    )(*args)


def _t5_bucket_np(rel):
    nb = REL_BUCKETS // 2
    max_exact = nb // 2
    n = np.abs(rel)
    thresholds = [int(math.ceil(max_exact * (REL_MAX_DIST / max_exact) ** (m / (nb - max_exact)) - 1e-9))
                  for m in range(1, nb - max_exact)]
    large = max_exact + sum((n >= th).astype(np.int64) for th in thresholds)
    return np.where(rel > 0, nb, 0) + np.where(n < max_exact, n, np.minimum(large, nb - 1))


def _bias_bucket_tables():
    kl = np.arange(KEY_TILE)[:, None]
    qpos = np.arange(2 * Q_BLOCK)[None, :]
    out = np.zeros((2, KEY_TILE, 2 * Q_BLOCK), np.int32)
    out[0] = _t5_bucket_np(kl - KEY_TILE - qpos)
    allowed = (kl // ATT_CHUNK) <= (qpos // ATT_CHUNK)
    out[1] = np.where(allowed, _t5_bucket_np(kl - qpos), -1)
    return out


def _bias_table_kernel(bucket_ref, relb_ref, out_ref):
    h = pl.program_id(0)
    bucket = bucket_ref[...]
    acc = jnp.where(bucket < 0, NEG_BIG, 0.0).astype(F32)
    for b in range(REL_BUCKETS):
        acc = jnp.where(bucket == b, relb_ref[b, h], acc)
    out_ref[...] = jnp.where(bucket < 0, NEG_BIG, acc * LOG2E)


def _bias_tables(rel_bias):
    nh = rel_bias.shape[1]
    buckets = jnp.asarray(_bias_bucket_tables())
    return pl.pallas_call(
        _bias_table_kernel,
        grid=(nh,),
        in_specs=[pl.BlockSpec(buckets.shape, lambda h: (0, 0, 0)),
                  pl.BlockSpec(memory_space=pltpu.SMEM)],
        out_specs=pl.BlockSpec((None,) + buckets.shape, lambda h: (h, 0, 0, 0)),
        out_shape=jax.ShapeDtypeStruct((nh,) + buckets.shape, F32),
        name="t5_bias_tables",
    )(buckets, rel_bias)


def _diff_attn_kernel(q_ref, k_ref, v_ref, tab_ref, relb_ref, lam_ref, sg_ref, o_ref, s_scr, p_scr, ve_scr, *,
                      lambda_init):
    t = q_ref.shape[0]
    qb, kt_, dh = Q_BLOCK, KEY_TILE, HEAD
    dv = 2 * dh
    nhead = q_ref.shape[1] // dv
    h0 = pl.program_id(1) * nhead
    lam = lam_ref[...]
    lam_full = (jnp.exp(jnp.sum(lam[0:1, :] * lam[1:2, :], axis=-1, keepdims=True))
                - jnp.exp(jnp.sum(lam[2:3, :] * lam[3:4, :], axis=-1, keepdims=True)) + lambda_init)
    qp = 2 * qb
    lane = _iota((qp, dv), 1)
    sg = sg_ref[...] * (1.0 - lambda_init)
    for hh in range(nhead):
        ve_scr[hh, :dv, :] = jnp.transpose(v_ref[:, hh * dv:(hh + 1) * dv].astype(F32)).astype(BF16)
        ve_scr[hh, dv:, :] = jnp.ones((ve_scr.shape[1] - dv, t), BF16)

    npairs = t // kt_
    col_max = {}
    dyn0 = jnp.minimum(pl.program_id(0), 0)

    def logits(idx, hh, jl):
        slot = idx % 2 + dyn0
        hs = slice(hh * dv, (hh + 1) * dv)
        c_far = relb_ref[REL_BUCKETS // 2 - 1, h0 + hh] * LOG2E
        q = q_ref[jl * qp:(jl + 1) * qp, hs]
        zero = jnp.zeros((), q.dtype)
        qm = jnp.concatenate([jnp.where(lane < dh, q, zero), jnp.where(lane >= dh, q, zero)], axis=0)
        mx_far = jnp.full((1, 2 * qp), NEG_BIG, F32)
        mx_near = jnp.full((1, 2 * qp), NEG_BIG, F32)
        for j in range(jl + 1):
            ks = slice(j * kt_, (j + 1) * kt_)
            s = _dot_nt(k_ref[ks, hs], qm)
            if j >= jl - 1:
                tab = tab_ref[hh, 1 if j == jl else 0]
                s = s + jnp.concatenate([tab, tab], axis=1)
                mx_near = jnp.maximum(mx_near, jnp.max(s, axis=0, keepdims=True))
            else:
                mx_far = jnp.maximum(mx_far, jnp.max(s, axis=0, keepdims=True))
            s_scr[slot, ks, :] = s
            yield
        col_max[idx] = (jnp.maximum(mx_near, mx_far + c_far), c_far)

    def outputs(idx, hh, jl):
        slot = idx % 2 + dyn0
        m, c_far = col_max.pop(idx)
        m_far = m - c_far
        for j in range(jl + 1):
            ks = slice(j * kt_, (j + 1) * kt_)
            p_scr[ks, :] = jnp.exp2(s_scr[slot, ks, :] - (m if j >= jl - 1 else m_far)).astype(BF16)
            yield
        kend = (jl + 1) * kt_
        acc = jnp.dot(ve_scr[hh, :, :kend], p_scr[:kend, :], preferred_element_type=F32)
        pv = acc[:dv, :] / acc[dv:dv + 1, :]
        o = pv[:, :qp] - lam_full * pv[:, qp:]
        o = o * lax.rsqrt(jnp.mean(o * o, axis=0, keepdims=True) + SUBLN_EPS)
        o_ref[jl * qp:(jl + 1) * qp, hh * dv:(hh + 1) * dv] = jnp.transpose(o * sg).astype(o_ref.dtype)

    def run(*gens):
        for _ in itertools.zip_longest(*gens):
            pass

    items = [(hh, jl) for hh in range(nhead) for jl in reversed(range(npairs))]
    run(logits(0, *items[0]))
    for idx, item in enumerate(items):
        nxt = [logits(idx + 1, *items[idx + 1])] if idx + 1 < len(items) else []
        run(outputs(idx, *item), *nxt)


def _diff_attn(q, k, v, tabs, rel_bias, lam, subln_g, lambda_init, heads_per_step=ATTN_HEADS_PER_STEP):
    bsz, t, d = q.shape
    nh = d // (2 * HEAD)
    hps = heads_per_step
    seq_spec = pl.BlockSpec((None, t, hps * 2 * HEAD), lambda b, h: (b, 0, h))
    return pl.pallas_call(
        functools.partial(_diff_attn_kernel, lambda_init=lambda_init),
        grid=(bsz, nh // hps),
        in_specs=[seq_spec, seq_spec, seq_spec,
                  pl.BlockSpec((hps,) + tabs.shape[1:], lambda b, h: (h, 0, 0, 0)),
                  pl.BlockSpec(memory_space=pltpu.SMEM),
                  pl.BlockSpec(lam.shape, lambda b, h: (0, 0)),
                  pl.BlockSpec(subln_g.shape, lambda b, h: (0, 0))],
        out_specs=seq_spec,
        out_shape=jax.ShapeDtypeStruct((bsz, t, d), BF16),
        scratch_shapes=[pltpu.VMEM((2, t, 4 * Q_BLOCK), F32),
                        pltpu.VMEM((t, 4 * Q_BLOCK), BF16),
                        pltpu.VMEM((hps, 2 * HEAD + 16, t), BF16)],
        compiler_params=pltpu.CompilerParams(dimension_semantics=("parallel", "parallel"),
                                             vmem_limit_bytes=VMEM_LIMIT),
        name="diff_attn",
    )(q, k, v, tabs, rel_bias, lam, subln_g)


def kernel(x, a_mu, a_w_r, a_w_k, a_w_v, a_w_o, a_w0, a_w1, a_w2, a_a0, a_a1, a_a2, a_g1, a_g2, a_k_k, a_k_a,
           a_r_k, a_lnx_g, a_lnx_b, b_w_kv, b_w_q, b_lam, b_subln_g, b_w_o, rel_bias, mlp_w1, mlp_w2, ln_g, ln_b):
    bsz, t, d = x.shape
    assert d % MXU_DIM == 0 and t % WKV_TIME_BLOCK == 0 and t % PROJ_ROWS == 0 and t % (2 * POST_SUB_ROWS) == 0
    bf = lambda w: w.astype(BF16)
    row = lambda w: w.reshape(1, d)

    p = dict(mu=a_mu[0], w_r=bf(a_w_r[0]), w_k=bf(a_w_k[0]), w_v=bf(a_w_v[0]), w0=row(a_w0[0]),
             w1=bf(a_w1[0]), w2=bf(a_w2[0]), a0=row(a_a0[0]), a1=bf(a_a1[0]), a2=bf(a_a2[0]),
             g1=bf(a_g1[0]), g2=bf(a_g2[0]), k_k=row(a_k_k[0]), k_a=row(a_k_a[0]), r_k=row(a_r_k[0]))
    rt, kt, bt, at, v, bonus, g, gl = _rwkv_proj(x, p)
    y = _wkv(rt, kt, bt, at, v, gl)
    ln0 = jnp.stack([ln_g[0, 0], ln_b[0, 0], ln_g[0, 1], ln_b[0, 1]])
    w1_all, w2_all = bf(mlp_w1), bf(mlp_w2)
    x, q, k_sh, v_sh = _post_block(
        x, y, dict(bonus=bonus, g=g, lnx_g=row(a_lnx_g[0]), lnx_b=row(a_lnx_b[0])),
        bf(a_w_o[0]), ln0, w1_all, w2_all, 0, dict(w_q=bf(b_w_q[0]), w_kv=bf(b_w_kv)))

    lambda_init = 0.8 - 0.6 * math.exp(-0.3 * 1)
    tabs = _bias_tables(rel_bias)
    o = _diff_attn(q, k_sh, v_sh, tabs, rel_bias, b_lam[0], b_subln_g[0].reshape(2 * HEAD, 1), lambda_init)
    ln1 = jnp.stack([ln_g[1, 0], ln_b[1, 0], ln_g[1, 1], ln_b[1, 1]])
    (x,) = _post_block(x, o, None, bf(b_w_o[0]), ln1, w1_all, w2_all, 1, None)
    return x
```

```python
import functools
import itertools
import math

import numpy as np
import jax
import jax.numpy as jnp
from jax import lax
from jax.experimental import pallas as pl
from jax.experimental.pallas import tpu as pltpu

F32 = jnp.float32
BF16 = jnp.bfloat16

DEPTH = 2
ALPHA = (2.0 * DEPTH) ** 0.25
HEAD = 64
GN_EPS = 64e-5
SUBLN_EPS = 1e-5
LN_EPS = 1e-5
REL_BUCKETS = 32
REL_MAX_DIST = 128
ATT_CHUNK = 64
Q_BLOCK = 128

LANES = 128
MXU_DIM = 256
WKV_L = 64
PACK = MXU_DIM // HEAD
KEY_TILE = 256

PROJ_ROWS = 512
POST_SUB_ROWS = 256
WKV_TIME_BLOCK = 1024
ATTN_HEADS_PER_STEP = 2
NEG_BIG = -1e30
LOG2E = math.log2(math.e)
VMEM_LIMIT = 56 * 1024 * 1024


def _dot(a, b):
    return jnp.dot(a.astype(BF16), b.astype(BF16), preferred_element_type=F32)


def _dot_nt(a, b):
    return lax.dot_general(a.astype(BF16), b.astype(BF16), (((1,), (1,)), ((), ())),
                           preferred_element_type=F32)


def _dot_tn(a, b):
    return lax.dot_general(a.astype(BF16), b.astype(BF16), (((0,), (0,)), ((), ())),
                           preferred_element_type=F32)


def _iota(shape, dim):
    return lax.broadcasted_iota(jnp.int32, shape, dim)


def _head_block_ones():
    r = _iota((MXU_DIM, MXU_DIM), 0) // HEAD
    c = _iota((MXU_DIM, MXU_DIM), 1) // HEAD
    return jnp.where(r == c, 1.0, 0.0).astype(BF16)


def _head_sum_bcast(x, ones_bd):
    d = x.shape[1]
    parts = [_dot(x[:, p:p + MXU_DIM], ones_bd) for p in range(0, d, MXU_DIM)]
    return jnp.concatenate(parts, axis=1)


def _run_staggered(gens):
    gens = list(gens)
    live = []
    while gens or live:
        if gens:
            live.append(gens.pop(0))
        for g in list(live):
            if next(g, StopIteration) is StopIteration:
                live.remove(g)


def _layer_norm(x, g, b, eps):
    mu = jnp.mean(x, axis=-1, keepdims=True)
    xc = x - mu
    var = jnp.mean(xc * xc, axis=-1, keepdims=True)
    return xc * lax.rsqrt(var + eps) * g + b


def _rwkv_proj_kernel(x_ref, xp_ref, mu_ref, wr_ref, wk_ref, wv_ref, w0_ref, w1_ref, w2_ref,
                      a0_ref, a1_ref, a2_ref, g1_ref, g2_ref, kk_ref, ka_ref, rk_ref,
                      rt_ref, kt_ref, bt_ref, at_ref, v_ref, bonus_ref, g_ref, gl_ref):
    i = pl.program_id(1)
    x = x_ref[...]
    tm, d = x.shape
    prev_row = jnp.where(i == 0, 0.0, xp_ref[7:8, :])
    row = _iota((tm, d), 0)
    xprev = jnp.where(row == 0, prev_row, pltpu.roll(x, 1, 0))
    xx = xprev - x
    mu = mu_ref[...]

    xr, xw, xk, xv, xa, xg = [(x + xx * mu[j:j + 1, :]).astype(BF16) for j in range(6)]
    hw = jnp.tanh(_dot(xw, w1_ref[...])).astype(BF16)
    ha = _dot(xa, a1_ref[...]).astype(BF16)
    hg = jax.nn.sigmoid(_dot(xg, g1_ref[...])).astype(BF16)

    ones_bd = _head_block_ones()
    tr = _iota((MXU_DIM, MXU_DIM), 0)
    tc = _iota((MXU_DIM, MXU_DIM), 1)
    tri = jnp.where((tr // WKV_L == tc // WKV_L) & (tc <= tr), 1.0, 0.0).astype(BF16)

    def col_group(c0):
        cl = slice(c0, c0 + MXU_DIM)
        r = _dot(xr, wr_ref[:, cl])
        k = _dot(xk, wk_ref[:, cl])
        v = _dot(xv, wv_ref[:, cl])
        wraw = w0_ref[:, cl] + _dot(hw, w2_ref[:, cl])
        a = jax.nn.sigmoid(a0_ref[:, cl] + _dot(ha, a2_ref[:, cl]))
        g = _dot(hg, g2_ref[:, cl])
        yield
        ld = -math.exp(-0.5) * jax.nn.sigmoid(wraw)
        kk = k * kk_ref[:, cl]
        kk = kk * jnp.minimum(lax.rsqrt(_dot(kk * kk, ones_bd)), 1e12)
        k = k * (1.0 + (a - 1.0) * ka_ref[:, cl])
        bonus = _dot(r * k * rk_ref[:, cl], ones_bd) * v
        ld_hi = ld.astype(BF16)
        ld_lo = (ld - ld_hi.astype(F32)).astype(BF16)
        cs = jnp.concatenate(
            [jnp.dot(tri, ld_hi[p:p + MXU_DIM, :], preferred_element_type=F32)
             + jnp.dot(tri, ld_lo[p:p + MXU_DIM, :], preferred_element_type=F32)
             for p in range(0, tm, MXU_DIM)], axis=0)
        yield
        igam = jnp.exp(-cs)
        rt_ref[:, cl] = (r * jnp.exp(cs)).astype(BF16)
        kt_ref[:, cl] = (k * igam).astype(BF16)
        bt_ref[:, cl] = (kk * a * igam).astype(BF16)
        at_ref[:, cl] = (-kk * jnp.exp(cs - ld)).astype(BF16)
        v_ref[:, cl] = v.astype(BF16)
        bonus_ref[:, cl] = bonus.astype(BF16)
        g_ref[:, cl] = g.astype(BF16)
        gl_ref[:, cl] = jnp.exp(cs.reshape(tm // WKV_L, WKV_L, MXU_DIM)[:, WKV_L - 1, :])

    _run_staggered([col_group(c0) for c0 in range(0, d, MXU_DIM)])


def _rwkv_proj(x, p, tm=PROJ_ROWS):
    bsz, t, d = x.shape
    nt = t // tm
    row_spec = pl.BlockSpec((None, tm, d), lambda b, i: (b, i, 0))
    prev_spec = pl.BlockSpec((None, 8, d), lambda b, i: (b, jnp.maximum(i * (tm // 8) - 1, 0), 0))

    def full(arr):
        nd = arr.ndim
        return pl.BlockSpec(arr.shape, lambda b, i: (0,) * nd, pipeline_mode=pl.Buffered(1))

    weights = [p["mu"], p["w_r"], p["w_k"], p["w_v"], p["w0"], p["w1"], p["w2"], p["a0"], p["a1"], p["a2"],
               p["g1"], p["g2"], p["k_k"], p["k_a"], p["r_k"]]
    out_bf = jax.ShapeDtypeStruct((bsz, t, d), BF16)
    out_shape = [out_bf] * 7 + [jax.ShapeDtypeStruct((bsz, t // WKV_L, d), F32)]
    out_specs = [row_spec] * 7 + [pl.BlockSpec((None, tm // WKV_L, d), lambda b, i: (b, i, 0))]
    return pl.pallas_call(
        _rwkv_proj_kernel,
        grid=(bsz, nt),
        in_specs=[row_spec, prev_spec] + [full(w) for w in weights],
        out_specs=out_specs,
        out_shape=out_shape,
        compiler_params=pltpu.CompilerParams(dimension_semantics=("parallel", "parallel"),
                                             vmem_limit_bytes=VMEM_LIMIT),
        name="rwkv_proj",
    )(x, x, *weights)


def _wkv_kernel(rt_ref, kt_ref, bt_ref, at_ref, v_ref, gl_ref, y_ref, st_ref, t_scr, rb_scr, x1_scr):
    tb, d = rt_ref.shape
    L, W = WKV_L, MXU_DIM
    npack = d // W
    row = _iota((L, W), 0)
    sidx = _iota((L, W), 1) % L
    strict = sidx < row
    incl = sidx <= row
    eye = sidx == row
    couple = {b: (row // (2 * b) == sidx // (2 * b)) & ((row // b) % 2 == 1) & ((sidx // b) % 2 == 0)
              for b in (1, 2, 4, 8, 16, 32)}
    r2 = _iota((W, W), 0)
    c2 = _iota((W, W), 1)
    bd = (r2 // L) == (c2 // L)
    diag = r2 == c2

    def stack(xb):
        return jnp.where(bd, jnp.concatenate([xb] * PACK, axis=0), jnp.zeros((), xb.dtype))

    @pl.when(pl.program_id(1) == 0)
    def _():
        st_ref[...] = jnp.zeros_like(st_ref)

    dyn0 = jnp.minimum(pl.program_id(0), 0)

    def rows(c):
        return pl.ds(pl.multiple_of(c * L, L), L)

    def prepare(c, pk, slot):
        slot = slot + dyn0
        sl = rows(c)
        ln = slice(pk * W, (pk + 1) * W)
        rt = rt_ref[sl, ln]
        kt = kt_ref[sl, ln]
        bt = bt_ref[sl, ln]
        at = at_ref[sl, ln]
        v = v_ref[sl, ln]
        gmat = _dot_nt(jnp.concatenate([at, rt], axis=0),
                       jnp.concatenate([stack(bt), stack(kt)], axis=0))
        yield
        ab = gmat[:L, :W]
        ak = gmat[:L, W:]
        rb = gmat[L:, :W]
        rk = gmat[L:, W:]
        tm_ = jnp.where(eye, 1.0, 0.0) + jnp.where(couple[1], ab, 0.0)
        res = _dot(jnp.concatenate([jnp.where(strict, ak, 0.0), jnp.where(incl, rk, 0.0)], axis=0), stack(v))
        yield
        for b in (2, 4, 8, 16, 32):
            ct = _dot(jnp.where(couple[b], ab, 0.0), stack(tm_.astype(BF16)))
            yield
            tm_ = tm_ + _dot(tm_, stack(ct.astype(BF16)))
            yield
        t_scr[slot, pk] = tm_.astype(BF16)
        rb_scr[slot, pk] = jnp.where(incl, rb, 0.0).astype(BF16)
        x1_scr[slot, pk] = res

    def advance(c, pk, slot):
        slot = slot + dyn0
        sl = rows(c)
        ln = slice(pk * W, (pk + 1) * W)
        tm_ = t_scr[slot, pk]
        rbi = rb_scr[slot, pk]
        x1r = x1_scr[slot, pk]
        rt = rt_ref[sl, ln]
        kt = kt_ref[sl, ln]
        bt = bt_ref[sl, ln]
        at = at_ref[sl, ln]
        v = v_ref[sl, ln]
        gl = gl_ref[pl.ds(c, 1), ln]
        st = st_ref[pk]
        ra = _dot(jnp.concatenate([rt, at], axis=0), st)
        yield
        ub = _dot(tm_, stack((x1r[:L] + ra[L:]).astype(BF16))).astype(BF16)
        yield
        yr = _dot(rbi, stack(ub))
        bh = (bt.astype(F32) * gl).astype(BF16)
        kh = (kt.astype(F32) * gl).astype(BF16)
        upd = _dot_tn(jnp.concatenate([bh, kh], axis=0), jnp.concatenate([ub, v], axis=0))
        yield
        glcol = jnp.sum(jnp.where(diag, gl, 0.0), axis=1, keepdims=True)
        st_ref[pk] = glcol * st + jnp.where(bd, upd, 0.0)
        y_ref[sl, ln] = (ra[:L] + yr + x1r[L:]).astype(y_ref.dtype)

    def run(*gens):
        for _ in itertools.zip_longest(*gens):
            pass

    npair = tb // (2 * L)
    packs = range(npack)

    def advance2(c, s0, s1, pk):
        return itertools.chain(advance(c, pk, s0), advance(c + 1, pk, s1))

    run(*([prepare(0, pk, 0) for pk in packs] + [prepare(1, pk, 1) for pk in packs]))

    def body(i, carry):
        c = 2 * i
        s0 = c % 4
        run(*([advance2(c, s0, s0 + 1, pk) for pk in packs]
              + [prepare(c + 2, pk, (s0 + 2) % 4) for pk in packs]
              + [prepare(c + 3, pk, (s0 + 3) % 4) for pk in packs]))
        return carry

    lax.fori_loop(0, npair - 1, body, 0, unroll=True)
    c_last = 2 * (npair - 1)
    run(*[advance2(c_last, c_last % 4, c_last % 4 + 1, pk) for pk in packs])


def _wkv(rt, kt, bt, at, v, gl, tb=WKV_TIME_BLOCK):
    bsz, t, d = rt.shape
    npack = d // MXU_DIM
    seq_spec = pl.BlockSpec((None, tb, d), lambda b, i: (b, i, 0))
    gl_spec = pl.BlockSpec((None, tb // WKV_L, d), lambda b, i: (b, i, 0))
    return pl.pallas_call(
        _wkv_kernel,
        grid=(bsz, t // tb),
        in_specs=[seq_spec] * 5 + [gl_spec],
        out_specs=seq_spec,
        out_shape=jax.ShapeDtypeStruct((bsz, t, d), BF16),
        scratch_shapes=[pltpu.VMEM((npack, MXU_DIM, MXU_DIM), F32),
                        pltpu.VMEM((4, npack, WKV_L, MXU_DIM), BF16),
                        pltpu.VMEM((4, npack, WKV_L, MXU_DIM), BF16),
                        pltpu.VMEM((4, npack, 2 * WKV_L, MXU_DIM), F32)],
        compiler_params=pltpu.CompilerParams(dimension_semantics=("parallel", "arbitrary"),
                                             vmem_limit_bytes=VMEM_LIMIT),
        name="wkv7",
    )(rt, kt, bt, at, v, gl)


def _post_block_kernel(*refs, rwkv, project, sub_rows):
    it = iter(refs)
    x_ref = next(it)
    y_ref = next(it)
    if rwkv:
        bonus_ref, g_ref, lnxg_ref, lnxb_ref = next(it), next(it), next(it), next(it)
    wo_ref, ln_ref, w1_ref, w2_ref = next(it), next(it), next(it), next(it)
    if project:
        wq_ref, wkv_ref = next(it), next(it)
    out_ref = next(it)
    if project:
        q_ref, k_ref, v_ref = next(it), next(it), next(it)

    tm, d = x_ref.shape
    ln = ln_ref[...]
    ones_bd = _head_block_ones() if rwkv else None

    def sub_tile(rs):
        x = x_ref[rs, :]
        if rwkv:
            y = y_ref[rs, :].astype(F32)
            mean = _head_sum_bcast(y, ones_bd) * (1.0 / HEAD)
            yc = y - mean
            var = _head_sum_bcast(yc * yc, ones_bd) * (1.0 / HEAD)
            yn = yc * lax.rsqrt(var + GN_EPS) * lnxg_ref[...] + lnxb_ref[...]
            mixed = ((yn + bonus_ref[rs, :].astype(F32)) * g_ref[rs, :].astype(F32)).astype(BF16)
        else:
            mixed = y_ref[rs, :]
        h = _dot(mixed, wo_ref[...])
        yield
        x1 = _layer_norm(ALPHA * x + h, ln[0:1, :], ln[1:2, :], LN_EPS)
        yield
        hid = jnp.maximum(_dot(x1, w1_ref[...]), 0.0)
        h2 = _dot(hid * hid, w2_ref[...])
        yield
        x2 = _layer_norm(ALPHA * x1 + h2, ln[2:3, :], ln[3:4, :], LN_EPS)
        out_ref[rs, :] = x2
        yield
        if project:
            x2b = x2.astype(BF16)
            q_ref[rs, :] = (_dot(x2b, wq_ref[...]) * (HEAD ** -0.5 * LOG2E)).astype(BF16)
            kv = _dot(x2b, wkv_ref[...])
            k_ref[rs, :] = kv[:, :d].astype(BF16)
            v_ref[rs, :] = kv[:, d:].astype(BF16)

    _run_staggered([sub_tile(slice(r, r + sub_rows)) for r in range(0, tm, sub_rows)])


def _post_block(x, y, rw, w_o, ln, w1, w2, layer, proj, sub_rows=POST_SUB_ROWS):
    bsz, t, d = x.shape
    rwkv = rw is not None
    project = proj is not None
    tm = 2 * sub_rows
    row_spec = pl.BlockSpec((None, tm, d), lambda b, i: (b, i, 0))

    def full(arr):
        nd = arr.ndim
        return pl.BlockSpec(arr.shape, lambda b, i: (0,) * nd, pipeline_mode=pl.Buffered(1))

    def slab(arr):
        return pl.BlockSpec((None,) + arr.shape[1:], lambda b, i: (layer, 0, 0), pipeline_mode=pl.Buffered(1))

    args = [x, y]
    specs = [row_spec, row_spec]
    if rwkv:
        args += [rw["bonus"], rw["g"], rw["lnx_g"], rw["lnx_b"]]
        specs += [row_spec, row_spec, full(rw["lnx_g"]), full(rw["lnx_b"])]
    args += [w_o, ln, w1, w2]
    specs += [full(w_o), full(ln), slab(w1), slab(w2)]
    out_shape = [jax.ShapeDtypeStruct((bsz, t, d), F32)]
    out_specs = [row_spec]
    if project:
        args += [proj["w_q"], proj["w_kv"]]
        specs += [full(proj["w_q"]), full(proj["w_kv"])]
        out_shape += [jax.ShapeDtypeStruct((bsz, t, d), BF16)] * 3
        out_specs += [row_spec] * 3
    return pl.pallas_call(
        functools.partial(_post_block_kernel, rwkv=rwkv, project=project, sub_rows=sub_rows),
        grid=(bsz, t // tm),
        in_specs=specs,
        out_specs=out_specs,
        out_shape=out_shape,
        compiler_params=pltpu.CompilerParams(dimension_semantics=("parallel", "parallel"),
                                             vmem_limit_bytes=VMEM_LIMIT),
        name="post_rwkv" if rwkv else "post_attn",
    )(*args)


def _t5_bucket_np(rel):
    nb = REL_BUCKETS // 2
    max_exact = nb // 2
    n = np.abs(rel)
    thresholds = [int(math.ceil(max_exact * (REL_MAX_DIST / max_exact) ** (m / (nb - max_exact)) - 1e-9))
                  for m in range(1, nb - max_exact)]
    large = max_exact + sum((n >= th).astype(np.int64) for th in thresholds)
    return np.where(rel > 0, nb, 0) + np.where(n < max_exact, n, np.minimum(large, nb - 1))


def _bias_bucket_tables():
    kl = np.arange(KEY_TILE)[:, None]
    qpos = np.arange(2 * Q_BLOCK)[None, :]
    out = np.zeros((2, KEY_TILE, 2 * Q_BLOCK), np.int32)
    out[0] = _t5_bucket_np(kl - KEY_TILE - qpos)
    allowed = (kl // ATT_CHUNK) <= (qpos // ATT_CHUNK)
    out[1] = np.where(allowed, _t5_bucket_np(kl - qpos), -1)
    return out


def _bias_table_kernel(bucket_ref, relb_ref, out_ref):
    h = pl.program_id(0)
    bucket = bucket_ref[...]
    acc = jnp.where(bucket < 0, NEG_BIG, 0.0).astype(F32)
    for b in range(REL_BUCKETS):
        acc = jnp.where(bucket == b, relb_ref[b, h], acc)
    out_ref[...] = jnp.where(bucket < 0, NEG_BIG, acc * LOG2E)


def _bias_tables(rel_bias):
    nh = rel_bias.shape[1]
    buckets = jnp.asarray(_bias_bucket_tables())
    return pl.pallas_call(
        _bias_table_kernel,
        grid=(nh,),
        in_specs=[pl.BlockSpec(buckets.shape, lambda h: (0, 0, 0)),
                  pl.BlockSpec(memory_space=pltpu.SMEM)],
        out_specs=pl.BlockSpec((None,) + buckets.shape, lambda h: (h, 0, 0, 0)),
        out_shape=jax.ShapeDtypeStruct((nh,) + buckets.shape, F32),
        name="t5_bias_tables",
    )(buckets, rel_bias)


def _diff_attn_kernel(q_ref, k_ref, v_ref, tab_ref, relb_ref, lam_ref, sg_ref, o_ref, s_scr, p_scr, ve_scr, *,
                      lambda_init):
    t = q_ref.shape[0]
    qb, kt_, dh = Q_BLOCK, KEY_TILE, HEAD
    dv = 2 * dh
    nhead = q_ref.shape[1] // dv
    h0 = pl.program_id(1) * nhead
    lam = lam_ref[...]
    lam_full = (jnp.exp(jnp.sum(lam[0:1, :] * lam[1:2, :], axis=-1, keepdims=True))
                - jnp.exp(jnp.sum(lam[2:3, :] * lam[3:4, :], axis=-1, keepdims=True)) + lambda_init)
    qp = 2 * qb
    lane = _iota((qp, dv), 1)
    sg = sg_ref[...] * (1.0 - lambda_init)
    for hh in range(nhead):
        ve_scr[hh, :dv, :] = jnp.transpose(v_ref[:, hh * dv:(hh + 1) * dv].astype(F32)).astype(BF16)
        ve_scr[hh, dv:, :] = jnp.ones((ve_scr.shape[1] - dv, t), BF16)

    npairs = t // kt_
    col_max = {}
    dyn0 = jnp.minimum(pl.program_id(0), 0)

    def logits(idx, hh, jl):
        slot = idx % 2 + dyn0
        hs = slice(hh * dv, (hh + 1) * dv)
        c_far = relb_ref[REL_BUCKETS // 2 - 1, h0 + hh] * LOG2E
        q = q_ref[jl * qp:(jl + 1) * qp, hs]
        zero = jnp.zeros((), q.dtype)
        qm = jnp.concatenate([jnp.where(lane < dh, q, zero), jnp.where(lane >= dh, q, zero)], axis=0)
        mx_far = jnp.full((1, 2 * qp), NEG_BIG, F32)
        mx_near = jnp.full((1, 2 * qp), NEG_BIG, F32)
        for j in range(jl + 1):
            ks = slice(j * kt_, (j + 1) * kt_)
            s = _dot_nt(k_ref[ks, hs], qm)
            if j >= jl - 1:
                tab = tab_ref[hh, 1 if j == jl else 0]
                s = s + jnp.concatenate([tab, tab], axis=1)
                mx_near = jnp.maximum(mx_near, jnp.max(s, axis=0, keepdims=True))
            else:
                mx_far = jnp.maximum(mx_far, jnp.max(s, axis=0, keepdims=True))
            s_scr[slot, ks, :] = s
            yield
        col_max[idx] = (jnp.maximum(mx_near, mx_far + c_far), c_far)

    def outputs(idx, hh, jl):
        slot = idx % 2 + dyn0
        m, c_far = col_max.pop(idx)
        m_far = m - c_far
        for j in range(jl + 1):
            ks = slice(j * kt_, (j + 1) * kt_)
            p_scr[ks, :] = jnp.exp2(s_scr[slot, ks, :] - (m if j >= jl - 1 else m_far)).astype(BF16)
            yield
        kend = (jl + 1) * kt_
        acc = jnp.dot(ve_scr[hh, :, :kend], p_scr[:kend, :], preferred_element_type=F32)
        pv = acc[:dv, :] / acc[dv:dv + 1, :]
        o = pv[:, :qp] - lam_full * pv[:, qp:]
        o = o * lax.rsqrt(jnp.mean(o * o, axis=0, keepdims=True) + SUBLN_EPS)
        o_ref[jl * qp:(jl + 1) * qp, hh * dv:(hh + 1) * dv] = jnp.transpose(o * sg).astype(o_ref.dtype)

    def run(*gens):
        for _ in itertools.zip_longest(*gens):
            pass

    items = [(hh, jl) for hh in range(nhead) for jl in reversed(range(npairs))]
    run(logits(0, *items[0]))
    for idx, item in enumerate(items):
        nxt = [logits(idx + 1, *items[idx + 1])] if idx + 1 < len(items) else []
        run(outputs(idx, *item), *nxt)


def _diff_attn(q, k, v, tabs, rel_bias, lam, subln_g, lambda_init, heads_per_step=ATTN_HEADS_PER_STEP):
    bsz, t, d = q.shape
    nh = d // (2 * HEAD)
    hps = heads_per_step
    seq_spec = pl.BlockSpec((None, t, hps * 2 * HEAD), lambda b, h: (b, 0, h))
    return pl.pallas_call(
        functools.partial(_diff_attn_kernel, lambda_init=lambda_init),
        grid=(bsz, nh // hps),
        in_specs=[seq_spec, seq_spec, seq_spec,
                  pl.BlockSpec((hps,) + tabs.shape[1:], lambda b, h: (h, 0, 0, 0)),
                  pl.BlockSpec(memory_space=pltpu.SMEM),
                  pl.BlockSpec(lam.shape, lambda b, h: (0, 0)),
                  pl.BlockSpec(subln_g.shape, lambda b, h: (0, 0))],
        out_specs=seq_spec,
        out_shape=jax.ShapeDtypeStruct((bsz, t, d), BF16),
        scratch_shapes=[pltpu.VMEM((2, t, 4 * Q_BLOCK), F32),
                        pltpu.VMEM((t, 4 * Q_BLOCK), BF16),
                        pltpu.VMEM((hps, 2 * HEAD + 16, t), BF16)],
        compiler_params=pltpu.CompilerParams(dimension_semantics=("parallel", "parallel"),
                                             vmem_limit_bytes=VMEM_LIMIT),
        name="diff_attn",
    )(q, k, v, tabs, rel_bias, lam, subln_g)


def kernel(x, a_mu, a_w_r, a_w_k, a_w_v, a_w_o, a_w0, a_w1, a_w2, a_a0, a_a1, a_a2, a_g1, a_g2, a_k_k, a_k_a,
           a_r_k, a_lnx_g, a_lnx_b, b_w_kv, b_w_q, b_lam, b_subln_g, b_w_o, rel_bias, mlp_w1, mlp_w2, ln_g, ln_b):
    bsz, t, d = x.shape
    assert d % MXU_DIM == 0 and t % WKV_TIME_BLOCK == 0 and t % PROJ_ROWS == 0 and t % (2 * POST_SUB_ROWS) == 0
    bf = lambda w: w.astype(BF16)
    row = lambda w: w.reshape(1, d)

    p = dict(mu=a_mu[0], w_r=bf(a_w_r[0]), w_k=bf(a_w_k[0]), w_v=bf(a_w_v[0]), w0=row(a_w0[0]),
             w1=bf(a_w1[0]), w2=bf(a_w2[0]), a0=row(a_a0[0]), a1=bf(a_a1[0]), a2=bf(a_a2[0]),
             g1=bf(a_g1[0]), g2=bf(a_g2[0]), k_k=row(a_k_k[0]), k_a=row(a_k_a[0]), r_k=row(a_r_k[0]))
    rt, kt, bt, at, v, bonus, g, gl = _rwkv_proj(x, p)
    y = _wkv(rt, kt, bt, at, v, gl)
    ln0 = jnp.stack([ln_g[0, 0], ln_b[0, 0], ln_g[0, 1], ln_b[0, 1]])
    w1_all, w2_all = bf(mlp_w1), bf(mlp_w2)
    x, q, k_sh, v_sh = _post_block(
        x, y, dict(bonus=bonus, g=g, lnx_g=row(a_lnx_g[0]), lnx_b=row(a_lnx_b[0])),
        bf(a_w_o[0]), ln0, w1_all, w2_all, 0, dict(w_q=bf(b_w_q[0]), w_kv=bf(b_w_kv)))

    lambda_init = 0.8 - 0.6 * math.exp(-0.3 * 1)
    tabs = _bias_tables(rel_bias)
    o = _diff_attn(q, k_sh, v_sh, tabs, rel_bias, b_lam[0], b_subln_g[0].reshape(2 * HEAD, 1), lambda_init)
    ln1 = jnp.stack([ln_g[1, 0], ln_b[1, 0], ln_g[1, 1], ln_b[1, 1]])
    (x,) = _post_block(x, o, None, bf(b_w_o[0]), ln1, w1_all, w2_all, 1, None)
    return x
```

```python
import functools
import itertools
import math

import numpy as np
import jax
import jax.numpy as jnp
from jax import lax
from jax.experimental import pallas as pl
from jax.experimental.pallas import tpu as pltpu

F32 = jnp.float32
BF16 = jnp.bfloat16

DEPTH = 2
ALPHA = (2.0 * DEPTH) ** 0.25
HEAD = 64
GN_EPS = 64e-5
SUBLN_EPS = 1e-5
LN_EPS = 1e-5
REL_BUCKETS = 32
REL_MAX_DIST = 128
ATT_CHUNK = 64
Q_BLOCK = 128

LANES = 128
MXU_DIM = 256
WKV_L = 64
PACK = MXU_DIM // HEAD
KEY_TILE = 256

PROJ_ROWS = 512
POST_SUB_ROWS = 256
WKV_TIME_BLOCK = 1024
ATTN_HEADS_PER_STEP = 2
NEG_BIG = -1e30
LOG2E = math.log2(math.e)
VMEM_LIMIT = 56 * 1024 * 1024


def _dot(a, b):
    return jnp.dot(a.astype(BF16), b.astype(BF16), preferred_element_type=F32)


def _dot_nt(a, b):
    return lax.dot_general(a.astype(BF16), b.astype(BF16), (((1,), (1,)), ((), ())),
                           preferred_element_type=F32)


def _dot_tn(a, b):
    return lax.dot_general(a.astype(BF16), b.astype(BF16), (((0,), (0,)), ((), ())),
                           preferred_element_type=F32)


def _iota(shape, dim):
    return lax.broadcasted_iota(jnp.int32, shape, dim)


def _head_block_ones():
    r = _iota((MXU_DIM, MXU_DIM), 0) // HEAD
    c = _iota((MXU_DIM, MXU_DIM), 1) // HEAD
    return jnp.where(r == c, 1.0, 0.0).astype(BF16)


def _head_sum_bcast(x, ones_bd):
    d = x.shape[1]
    parts = [_dot(x[:, p:p + MXU_DIM], ones_bd) for p in range(0, d, MXU_DIM)]
    return jnp.concatenate(parts, axis=1)


def _run_staggered(gens):
    gens = list(gens)
    live = []
    while gens or live:
        if gens:
            live.append(gens.pop(0))
        for g in list(live):
            if next(g, StopIteration) is StopIteration:
                live.remove(g)


def _layer_norm(x, g, b, eps):
    mu = jnp.mean(x, axis=-1, keepdims=True)
    xc = x - mu
    var = jnp.mean(xc * xc, axis=-1, keepdims=True)
    return xc * lax.rsqrt(var + eps) * g + b


def _rwkv_proj_kernel(x_ref, xp_ref, mu_ref, wr_ref, wk_ref, wv_ref, w0_ref, w1_ref, w2_ref,
                      a0_ref, a1_ref, a2_ref, g1_ref, g2_ref, kk_ref, ka_ref, rk_ref,
                      rt_ref, kt_ref, bt_ref, at_ref, v_ref, bonus_ref, g_ref, gl_ref):
    i = pl.program_id(1)
    x = x_ref[...]
    tm, d = x.shape
    prev_row = jnp.where(i == 0, 0.0, xp_ref[7:8, :])
    row = _iota((tm, d), 0)
    xprev = jnp.where(row == 0, prev_row, pltpu.roll(x, 1, 0))
    xx = xprev - x
    mu = mu_ref[...]

    xr, xw, xk, xv, xa, xg = [(x + xx * mu[j:j + 1, :]).astype(BF16) for j in range(6)]
    hw = jnp.tanh(_dot(xw, w1_ref[...])).astype(BF16)
    ha = _dot(xa, a1_ref[...]).astype(BF16)
    hg = jax.nn.sigmoid(_dot(xg, g1_ref[...])).astype(BF16)

    ones_bd = _head_block_ones()
    tr = _iota((MXU_DIM, MXU_DIM), 0)
    tc = _iota((MXU_DIM, MXU_DIM), 1)
    tri = jnp.where((tr // WKV_L == tc // WKV_L) & (tc <= tr), 1.0, 0.0).astype(BF16)

    def col_group(c0):
        cl = slice(c0, c0 + MXU_DIM)
        r = _dot(xr, wr_ref[:, cl])
        k = _dot(xk, wk_ref[:, cl])
        v = _dot(xv, wv_ref[:, cl])
        wraw = w0_ref[:, cl] + _dot(hw, w2_ref[:, cl])
        a = jax.nn.sigmoid(a0_ref[:, cl] + _dot(ha, a2_ref[:, cl]))
        g = _dot(hg, g2_ref[:, cl])
        yield
        ld = -math.exp(-0.5) * jax.nn.sigmoid(wraw)
        kk = k * kk_ref[:, cl]
        kk = kk * jnp.minimum(lax.rsqrt(_dot(kk * kk, ones_bd)), 1e12)
        k = k * (1.0 + (a - 1.0) * ka_ref[:, cl])
        bonus = _dot(r * k * rk_ref[:, cl], ones_bd) * v
        ld_hi = ld.astype(BF16)
        ld_lo = (ld - ld_hi.astype(F32)).astype(BF16)
        cs = jnp.concatenate(
            [jnp.dot(tri, ld_hi[p:p + MXU_DIM, :], preferred_element_type=F32)
             + jnp.dot(tri, ld_lo[p:p + MXU_DIM, :], preferred_element_type=F32)
             for p in range(0, tm, MXU_DIM)], axis=0)
        yield
        igam = jnp.exp(-cs)
        rt_ref[:, cl] = (r * jnp.exp(cs)).astype(BF16)
        kt_ref[:, cl] = (k * igam).astype(BF16)
        bt_ref[:, cl] = (kk * a * igam).astype(BF16)
        at_ref[:, cl] = (-kk * jnp.exp(cs - ld)).astype(BF16)
        v_ref[:, cl] = v.astype(BF16)
        bonus_ref[:, cl] = bonus.astype(BF16)
        g_ref[:, cl] = g.astype(BF16)
        gl_ref[:, cl] = jnp.exp(cs.reshape(tm // WKV_L, WKV_L, MXU_DIM)[:, WKV_L - 1, :])

    _run_staggered([col_group(c0) for c0 in range(0, d, MXU_DIM)])


def _rwkv_proj(x, p, tm=PROJ_ROWS):
    bsz, t, d = x.shape
    nt = t // tm
    row_spec = pl.BlockSpec((None, tm, d), lambda b, i: (b, i, 0))
    prev_spec = pl.BlockSpec((None, 8, d), lambda b, i: (b, jnp.maximum(i * (tm // 8) - 1, 0), 0))

    def full(arr):
        nd = arr.ndim
        return pl.BlockSpec(arr.shape, lambda b, i: (0,) * nd, pipeline_mode=pl.Buffered(1))

    weights = [p["mu"], p["w_r"], p["w_k"], p["w_v"], p["w0"], p["w1"], p["w2"], p["a0"], p["a1"], p["a2"],
               p["g1"], p["g2"], p["k_k"], p["k_a"], p["r_k"]]
    out_bf = jax.ShapeDtypeStruct((bsz, t, d), BF16)
    out_shape = [out_bf] * 7 + [jax.ShapeDtypeStruct((bsz, t // WKV_L, d), F32)]
    out_specs = [row_spec] * 7 + [pl.BlockSpec((None, tm // WKV_L, d), lambda b, i: (b, i, 0))]
    return pl.pallas_call(
        _rwkv_proj_kernel,
        grid=(bsz, nt),
        in_specs=[row_spec, prev_spec] + [full(w) for w in weights],
        out_specs=out_specs,
        out_shape=out_shape,
        compiler_params=pltpu.CompilerParams(dimension_semantics=("parallel", "parallel"),
                                             vmem_limit_bytes=VMEM_LIMIT),
        name="rwkv_proj",
    )(x, x, *weights)


def _wkv_kernel(rt_ref, kt_ref, bt_ref, at_ref, v_ref, gl_ref, y_ref, st_ref, t_scr, rb_scr, x1_scr):
    tb, d = rt_ref.shape
    L, W = WKV_L, MXU_DIM
    npack = d // W
    row = _iota((L, W), 0)
    sidx = _iota((L, W), 1) % L
    strict = sidx < row
    incl = sidx <= row
    eye = sidx == row
    couple = {b: (row // (2 * b) == sidx // (2 * b)) & ((row // b) % 2 == 1) & ((sidx // b) % 2 == 0)
              for b in (8, 16, 32)}
    diag8 = strict & (row // 8 == sidx // 8)
    r2 = _iota((W, W), 0)
    c2 = _iota((W, W), 1)
    bd = (r2 // L) == (c2 // L)
    diag = r2 == c2

    def stack(xb):
        return jnp.where(bd, jnp.concatenate([xb] * PACK, axis=0), jnp.zeros((), xb.dtype))

    @pl.when(pl.program_id(1) == 0)
    def _():
        st_ref[...] = jnp.zeros_like(st_ref)

    dyn0 = jnp.minimum(pl.program_id(0), 0)

    def rows(c):
        return pl.ds(pl.multiple_of(c * L, L), L)

    def prepare(c, pk, slot):
        slot = slot + dyn0
        sl = rows(c)
        ln = slice(pk * W, (pk + 1) * W)
        rt = rt_ref[sl, ln]
        kt = kt_ref[sl, ln]
        bt = bt_ref[sl, ln]
        at = at_ref[sl, ln]
        v = v_ref[sl, ln]
        gmat = _dot_nt(jnp.concatenate([at, rt], axis=0),
                       jnp.concatenate([stack(bt), stack(kt)], axis=0))
        yield
        ab = gmat[:L, :W]
        ak = gmat[:L, W:]
        rb = gmat[L:, :W]
        rk = gmat[L:, W:]
        nd = jnp.where(diag8, ab, 0.0)
        tm_ = jnp.where(eye, 1.0, 0.0) + nd
        cur = nd.astype(BF16)
        cur = _dot(cur, stack(cur)).astype(BF16)
        res = _dot(jnp.concatenate([jnp.where(strict, ak, 0.0), jnp.where(incl, rk, 0.0)], axis=0), stack(v))
        yield
        nres = _dot(jnp.concatenate([cur, tm_.astype(BF16)], axis=0), stack(cur))
        tm_ = tm_ + nres[L:]
        yield
        tm_ = tm_ + _dot(tm_, stack(nres[:L].astype(BF16)))
        yield
        for b in (8, 16, 32):
            ct = _dot(jnp.where(couple[b], ab, 0.0), stack(tm_.astype(BF16)))
            yield
            tm_ = tm_ + _dot(tm_, stack(ct.astype(BF16)))
            yield
        t_scr[slot, pk] = tm_.astype(BF16)
        rb_scr[slot, pk] = jnp.where(incl, rb, 0.0).astype(BF16)
        x1_scr[slot, pk] = res

    def advance(c, pk, slot):
        slot = slot + dyn0
        sl = rows(c)
        ln = slice(pk * W, (pk + 1) * W)
        tm_ = t_scr[slot, pk]
        rbi = rb_scr[slot, pk]
        x1r = x1_scr[slot, pk]
        rt = rt_ref[sl, ln]
        kt = kt_ref[sl, ln]
        bt = bt_ref[sl, ln]
        at = at_ref[sl, ln]
        v = v_ref[sl, ln]
        gl = gl_ref[pl.ds(c, 1), ln]
        st = st_ref[pk]
        ra = _dot(jnp.concatenate([rt, at], axis=0), st)
        yield
        ub = _dot(tm_, stack((x1r[:L] + ra[L:]).astype(BF16))).astype(BF16)
        yield
        yr = _dot(rbi, stack(ub))
        bh = (bt.astype(F32) * gl).astype(BF16)
        kh = (kt.astype(F32) * gl).astype(BF16)
        upd = _dot_tn(jnp.concatenate([bh, kh], axis=0), jnp.concatenate([ub, v], axis=0))
        yield
        glcol = jnp.sum(jnp.where(diag, gl, 0.0), axis=1, keepdims=True)
        st_ref[pk] = glcol * st + jnp.where(bd, upd, 0.0)
        y_ref[sl, ln] = (ra[:L] + yr + x1r[L:]).astype(y_ref.dtype)

    def run(*gens):
        for _ in itertools.zip_longest(*gens):
            pass

    npair = tb // (2 * L)
    packs = range(npack)

    def advance2(c, s0, s1, pk):
        return itertools.chain(advance(c, pk, s0), advance(c + 1, pk, s1))

    run(*([prepare(0, pk, 0) for pk in packs] + [prepare(1, pk, 1) for pk in packs]))

    def body(i, carry):
        c = 2 * i
        s0 = c % 4
        run(*([advance2(c, s0, s0 + 1, pk) for pk in packs]
              + [prepare(c + 2, pk, (s0 + 2) % 4) for pk in packs]
              + [prepare(c + 3, pk, (s0 + 3) % 4) for pk in packs]))
        return carry

    lax.fori_loop(0, npair - 1, body, 0, unroll=True)
    c_last = 2 * (npair - 1)
    run(*[advance2(c_last, c_last % 4, c_last % 4 + 1, pk) for pk in packs])


def _wkv(rt, kt, bt, at, v, gl, tb=WKV_TIME_BLOCK):
    bsz, t, d = rt.shape
    npack = d // MXU_DIM
    seq_spec = pl.BlockSpec((None, tb, d), lambda b, i: (b, i, 0))
    gl_spec = pl.BlockSpec((None, tb // WKV_L, d), lambda b, i: (b, i, 0))
    return pl.pallas_call(
        _wkv_kernel,
        grid=(bsz, t // tb),
        in_specs=[seq_spec] * 5 + [gl_spec],
        out_specs=seq_spec,
        out_shape=jax.ShapeDtypeStruct((bsz, t, d), BF16),
        scratch_shapes=[pltpu.VMEM((npack, MXU_DIM, MXU_DIM), F32),
                        pltpu.VMEM((4, npack, WKV_L, MXU_DIM), BF16),
                        pltpu.VMEM((4, npack, WKV_L, MXU_DIM), BF16),
                        pltpu.VMEM((4, npack, 2 * WKV_L, MXU_DIM), F32)],
        compiler_params=pltpu.CompilerParams(dimension_semantics=("parallel", "arbitrary"),
                                             vmem_limit_bytes=VMEM_LIMIT),
        name="wkv7",
    )(rt, kt, bt, at, v, gl)


def _post_block_kernel(*refs, rwkv, project, sub_rows):
    it = iter(refs)
    x_ref = next(it)
    y_ref = next(it)
    if rwkv:
        bonus_ref, g_ref, lnxg_ref, lnxb_ref = next(it), next(it), next(it), next(it)
    wo_ref, ln_ref, w1_ref, w2_ref = next(it), next(it), next(it), next(it)
    if project:
        wq_ref, wkv_ref = next(it), next(it)
    out_ref = next(it)
    if project:
        q_ref, k_ref, v_ref = next(it), next(it), next(it)

    tm, d = x_ref.shape
    ln = ln_ref[...]
    ones_bd = _head_block_ones() if rwkv else None

    def sub_tile(rs):
        x = x_ref[rs, :]
        if rwkv:
            y = y_ref[rs, :].astype(F32)
            mean = _head_sum_bcast(y, ones_bd) * (1.0 / HEAD)
            yc = y - mean
            var = _head_sum_bcast(yc * yc, ones_bd) * (1.0 / HEAD)
            yn = yc * lax.rsqrt(var + GN_EPS) * lnxg_ref[...] + lnxb_ref[...]
            mixed = ((yn + bonus_ref[rs, :].astype(F32)) * g_ref[rs, :].astype(F32)).astype(BF16)
        else:
            mixed = y_ref[rs, :]
        h = _dot(mixed, wo_ref[...])
        yield
        x1 = _layer_norm(ALPHA * x + h, ln[0:1, :], ln[1:2, :], LN_EPS)
        yield
        hid = jnp.maximum(_dot(x1, w1_ref[...]), 0.0)
        h2 = _dot(hid * hid, w2_ref[...])
        yield
        x2 = _layer_norm(ALPHA * x1 + h2, ln[2:3, :], ln[3:4, :], LN_EPS)
        out_ref[rs, :] = x2
        yield
        if project:
            x2b = x2.astype(BF16)
            q_ref[rs, :] = (_dot(x2b, wq_ref[...]) * (HEAD ** -0.5 * LOG2E)).astype(BF16)
            kv = _dot(x2b, wkv_ref[...])
            k_ref[rs, :] = kv[:, :d].astype(BF16)
            v_ref[rs, :] = kv[:, d:].astype(BF16)

    _run_staggered([sub_tile(slice(r, r + sub_rows)) for r in range(0, tm, sub_rows)])


def _post_block(x, y, rw, w_o, ln, w1, w2, layer, proj, sub_rows=POST_SUB_ROWS):
    bsz, t, d = x.shape
    rwkv = rw is not None
    project = proj is not None
    tm = 2 * sub_rows
    row_spec = pl.BlockSpec((None, tm, d), lambda b, i: (b, i, 0))

    def full(arr):
        nd = arr.ndim
        return pl.BlockSpec(arr.shape, lambda b, i: (0,) * nd, pipeline_mode=pl.Buffered(1))

    def slab(arr):
        return pl.BlockSpec((None,) + arr.shape[1:], lambda b, i: (layer, 0, 0), pipeline_mode=pl.Buffered(1))

    args = [x, y]
    specs = [row_spec, row_spec]
    if rwkv:
        args += [rw["bonus"], rw["g"], rw["lnx_g"], rw["lnx_b"]]
        specs += [row_spec, row_spec, full(rw["lnx_g"]), full(rw["lnx_b"])]
    args += [w_o, ln, w1, w2]
    specs += [full(w_o), full(ln), slab(w1), slab(w2)]
    out_shape = [jax.ShapeDtypeStruct((bsz, t, d), F32)]
    out_specs = [row_spec]
    if project:
        args += [proj["w_q"], proj["w_kv"]]
        specs += [full(proj["w_q"]), full(proj["w_kv"])]
        out_shape += [jax.ShapeDtypeStruct((bsz, t, d), BF16)] * 3
        out_specs += [row_spec] * 3
    return pl.pallas_call(
        functools.partial(_post_block_kernel, rwkv=rwkv, project=project, sub_rows=sub_rows),
        grid=(bsz, t // tm),
        in_specs=specs,
        out_specs=out_specs,
        out_shape=out_shape,
        compiler_params=pltpu.CompilerParams(dimension_semantics=("parallel", "parallel"),
                                             vmem_limit_bytes=VMEM_LIMIT),
        name="post_rwkv" if rwkv else "post_attn",
    )(*args)


def _t5_bucket_np(rel):
    nb = REL_BUCKETS // 2
    max_exact = nb // 2
    n = np.abs(rel)
    thresholds = [int(math.ceil(max_exact * (REL_MAX_DIST / max_exact) ** (m / (nb - max_exact)) - 1e-9))
                  for m in range(1, nb - max_exact)]
    large = max_exact + sum((n >= th).astype(np.int64) for th in thresholds)
    return np.where(rel > 0, nb, 0) + np.where(n < max_exact, n, np.minimum(large, nb - 1))


def _bias_bucket_tables():
    kl = np.arange(KEY_TILE)[:, None]
    qpos = np.arange(2 * Q_BLOCK)[None, :]
    out = np.zeros((2, KEY_TILE, 2 * Q_BLOCK), np.int32)
    out[0] = _t5_bucket_np(kl - KEY_TILE - qpos)
    allowed = (kl // ATT_CHUNK) <= (qpos // ATT_CHUNK)
    out[1] = np.where(allowed, _t5_bucket_np(kl - qpos), -1)
    return out


def _bias_table_kernel(bucket_ref, relb_ref, out_ref):
    h = pl.program_id(0)
    bucket = bucket_ref[...]
    acc = jnp.where(bucket < 0, NEG_BIG, 0.0).astype(F32)
    for b in range(REL_BUCKETS):
        acc = jnp.where(bucket == b, relb_ref[b, h], acc)
    out_ref[...] = jnp.where(bucket < 0, NEG_BIG, acc * LOG2E)


def _bias_tables(rel_bias):
    nh = rel_bias.shape[1]
    buckets = jnp.asarray(_bias_bucket_tables())
    return pl.pallas_call(
        _bias_table_kernel,
        grid=(nh,),
        in_specs=[pl.BlockSpec(buckets.shape, lambda h: (0, 0, 0)),
                  pl.BlockSpec(memory_space=pltpu.SMEM)],
        out_specs=pl.BlockSpec((None,) + buckets.shape, lambda h: (h, 0, 0, 0)),
        out_shape=jax.ShapeDtypeStruct((nh,) + buckets.shape, F32),
        name="t5_bias_tables",
    )(buckets, rel_bias)


def _diff_attn_kernel(q_ref, k_ref, v_ref, tab_ref, relb_ref, lam_ref, sg_ref, o_ref, s_scr, p_scr, ve_scr, *,
                      lambda_init):
    t = q_ref.shape[0]
    qb, kt_, dh = Q_BLOCK, KEY_TILE, HEAD
    dv = 2 * dh
    nhead = q_ref.shape[1] // dv
    h0 = pl.program_id(1) * nhead
    lam = lam_ref[...]
    lam_full = (jnp.exp(jnp.sum(lam[0:1, :] * lam[1:2, :], axis=-1, keepdims=True))
                - jnp.exp(jnp.sum(lam[2:3, :] * lam[3:4, :], axis=-1, keepdims=True)) + lambda_init)
    qp = 2 * qb
    lane = _iota((qp, dv), 1)
    sg = sg_ref[...] * (1.0 - lambda_init)
    for hh in range(nhead):
        ve_scr[hh, :dv, :] = jnp.transpose(v_ref[:, hh * dv:(hh + 1) * dv].astype(F32)).astype(BF16)
        ve_scr[hh, dv:, :] = jnp.ones((ve_scr.shape[1] - dv, t), BF16)

    npairs = t // kt_
    col_max = {}
    dyn0 = jnp.minimum(pl.program_id(0), 0)

    def logits(idx, hh, jl):
        slot = idx % 2 + dyn0
        hs = slice(hh * dv, (hh + 1) * dv)
        c_far = relb_ref[REL_BUCKETS // 2 - 1, h0 + hh] * LOG2E
        q = q_ref[jl * qp:(jl + 1) * qp, hs]
        zero = jnp.zeros((), q.dtype)
        qm = jnp.concatenate([jnp.where(lane < dh, q, zero), jnp.where(lane >= dh, q, zero)], axis=0)
        mx_far = jnp.full((1, 2 * qp), NEG_BIG, F32)
        mx_near = jnp.full((1, 2 * qp), NEG_BIG, F32)
        for j in range(jl + 1):
            ks = slice(j * kt_, (j + 1) * kt_)
            s = _dot_nt(k_ref[ks, hs], qm)
            if j >= jl - 1:
                tab = tab_ref[hh, 1 if j == jl else 0]
                s = s + jnp.concatenate([tab, tab], axis=1)
                mx_near = jnp.maximum(mx_near, jnp.max(s, axis=0, keepdims=True))
            else:
                mx_far = jnp.maximum(mx_far, jnp.max(s, axis=0, keepdims=True))
            s_scr[slot, ks, :] = s
            yield
        col_max[idx] = (jnp.maximum(mx_near, mx_far + c_far), c_far)

    def outputs(idx, hh, jl):
        slot = idx % 2 + dyn0
        m, c_far = col_max.pop(idx)
        m_far = m - c_far
        for j in range(jl + 1):
            ks = slice(j * kt_, (j + 1) * kt_)
            p_scr[ks, :] = jnp.exp2(s_scr[slot, ks, :] - (m if j >= jl - 1 else m_far)).astype(BF16)
            yield
        kend = (jl + 1) * kt_
        acc = jnp.dot(ve_scr[hh, :, :kend], p_scr[:kend, :], preferred_element_type=F32)
        pv = acc[:dv, :] / acc[dv:dv + 1, :]
        o = pv[:, :qp] - lam_full * pv[:, qp:]
        o = o * lax.rsqrt(jnp.mean(o * o, axis=0, keepdims=True) + SUBLN_EPS)
        o_ref[jl * qp:(jl + 1) * qp, hh * dv:(hh + 1) * dv] = jnp.transpose(o * sg).astype(o_ref.dtype)

    def run(*gens):
        for _ in itertools.zip_longest(*gens):
            pass

    items = [(hh, jl) for hh in range(nhead) for jl in reversed(range(npairs))]
    run(logits(0, *items[0]))
    for idx, item in enumerate(items):
        nxt = [logits(idx + 1, *items[idx + 1])] if idx + 1 < len(items) else []
        run(outputs(idx, *item), *nxt)


def _diff_attn(q, k, v, tabs, rel_bias, lam, subln_g, lambda_init, heads_per_step=ATTN_HEADS_PER_STEP):
    bsz, t, d = q.shape
    nh = d // (2 * HEAD)
    hps = heads_per_step
    seq_spec = pl.BlockSpec((None, t, hps * 2 * HEAD), lambda b, h: (b, 0, h))
    return pl.pallas_call(
        functools.partial(_diff_attn_kernel, lambda_init=lambda_init),
        grid=(bsz, nh // hps),
        in_specs=[seq_spec, seq_spec, seq_spec,
                  pl.BlockSpec((hps,) + tabs.shape[1:], lambda b, h: (h, 0, 0, 0)),
                  pl.BlockSpec(memory_space=pltpu.SMEM),
                  pl.BlockSpec(lam.shape, lambda b, h: (0, 0)),
                  pl.BlockSpec(subln_g.shape, lambda b, h: (0, 0))],
        out_specs=seq_spec,
        out_shape=jax.ShapeDtypeStruct((bsz, t, d), BF16),
        scratch_shapes=[pltpu.VMEM((2, t, 4 * Q_BLOCK), F32),
                        pltpu.VMEM((t, 4 * Q_BLOCK), BF16),
                        pltpu.VMEM((hps, 2 * HEAD + 16, t), BF16)],
        compiler_params=pltpu.CompilerParams(dimension_semantics=("parallel", "parallel"),
                                             vmem_limit_bytes=VMEM_LIMIT),
        name="diff_attn",
    )(q, k, v, tabs, rel_bias, lam, subln_g)


def kernel(x, a_mu, a_w_r, a_w_k, a_w_v, a_w_o, a_w0, a_w1, a_w2, a_a0, a_a1, a_a2, a_g1, a_g2, a_k_k, a_k_a,
           a_r_k, a_lnx_g, a_lnx_b, b_w_kv, b_w_q, b_lam, b_subln_g, b_w_o, rel_bias, mlp_w1, mlp_w2, ln_g, ln_b):
    bsz, t, d = x.shape
    assert d % MXU_DIM == 0 and t % WKV_TIME_BLOCK == 0 and t % PROJ_ROWS == 0 and t % (2 * POST_SUB_ROWS) == 0
    bf = lambda w: w.astype(BF16)
    row = lambda w: w.reshape(1, d)

    p = dict(mu=a_mu[0], w_r=bf(a_w_r[0]), w_k=bf(a_w_k[0]), w_v=bf(a_w_v[0]), w0=row(a_w0[0]),
             w1=bf(a_w1[0]), w2=bf(a_w2[0]), a0=row(a_a0[0]), a1=bf(a_a1[0]), a2=bf(a_a2[0]),
             g1=bf(a_g1[0]), g2=bf(a_g2[0]), k_k=row(a_k_k[0]), k_a=row(a_k_a[0]), r_k=row(a_r_k[0]))
    rt, kt, bt, at, v, bonus, g, gl = _rwkv_proj(x, p)
    y = _wkv(rt, kt, bt, at, v, gl)
    ln0 = jnp.stack([ln_g[0, 0], ln_b[0, 0], ln_g[0, 1], ln_b[0, 1]])
    w1_all, w2_all = bf(mlp_w1), bf(mlp_w2)
    x, q, k_sh, v_sh = _post_block(
        x, y, dict(bonus=bonus, g=g, lnx_g=row(a_lnx_g[0]), lnx_b=row(a_lnx_b[0])),
        bf(a_w_o[0]), ln0, w1_all, w2_all, 0, dict(w_q=bf(b_w_q[0]), w_kv=bf(b_w_kv)))

    lambda_init = 0.8 - 0.6 * math.exp(-0.3 * 1)
    tabs = _bias_tables(rel_bias)
    o = _diff_attn(q, k_sh, v_sh, tabs, rel_bias, b_lam[0], b_subln_g[0].reshape(2 * HEAD, 1), lambda_init)
    ln1 = jnp.stack([ln_g[1, 0], ln_b[1, 0], ln_g[1, 1], ln_b[1, 1]])
    (x,) = _post_block(x, o, None, bf(b_w_o[0]), ln1, w1_all, w2_all, 1, None)
    return x
```

```python
import functools
import itertools
import math

import numpy as np
import jax
import jax.numpy as jnp
from jax import lax
from jax.experimental import pallas as pl
from jax.experimental.pallas import tpu as pltpu

F32 = jnp.float32
BF16 = jnp.bfloat16

DEPTH = 2
ALPHA = (2.0 * DEPTH) ** 0.25
HEAD = 64
GN_EPS = 64e-5
SUBLN_EPS = 1e-5
LN_EPS = 1e-5
REL_BUCKETS = 32
REL_MAX_DIST = 128
ATT_CHUNK = 64
Q_BLOCK = 128

LANES = 128
MXU_DIM = 256
WKV_L = 64
PACK = MXU_DIM // HEAD
KEY_TILE = 256

PROJ_ROWS = 512
POST_SUB_ROWS = 256
WKV_TIME_BLOCK = 1024
ATTN_HEADS_PER_STEP = 2
NEG_BIG = -1e30
LOG2E = math.log2(math.e)
VMEM_LIMIT = 56 * 1024 * 1024


def _dot(a, b):
    return jnp.dot(a.astype(BF16), b.astype(BF16), preferred_element_type=F32)


def _dot_nt(a, b):
    return lax.dot_general(a.astype(BF16), b.astype(BF16), (((1,), (1,)), ((), ())),
                           preferred_element_type=F32)


def _dot_tn(a, b):
    return lax.dot_general(a.astype(BF16), b.astype(BF16), (((0,), (0,)), ((), ())),
                           preferred_element_type=F32)


def _iota(shape, dim):
    return lax.broadcasted_iota(jnp.int32, shape, dim)


def _head_block_ones():
    r = _iota((MXU_DIM, MXU_DIM), 0) // HEAD
    c = _iota((MXU_DIM, MXU_DIM), 1) // HEAD
    return jnp.where(r == c, 1.0, 0.0).astype(BF16)


def _head_sum_bcast(x, ones_bd):
    d = x.shape[1]
    parts = [_dot(x[:, p:p + MXU_DIM], ones_bd) for p in range(0, d, MXU_DIM)]
    return jnp.concatenate(parts, axis=1)


def _run_staggered(gens):
    gens = list(gens)
    live = []
    while gens or live:
        if gens:
            live.append(gens.pop(0))
        for g in list(live):
            if next(g, StopIteration) is StopIteration:
                live.remove(g)


def _layer_norm(x, g, b, eps):
    mu = jnp.mean(x, axis=-1, keepdims=True)
    xc = x - mu
    var = jnp.mean(xc * xc, axis=-1, keepdims=True)
    return xc * lax.rsqrt(var + eps) * g + b


def _rwkv_proj_kernel(x_ref, xp_ref, mu_ref, wr_ref, wk_ref, wv_ref, w0_ref, w1_ref, w2_ref,
                      a0_ref, a1_ref, a2_ref, g1_ref, g2_ref, kk_ref, ka_ref, rk_ref,
                      rt_ref, kt_ref, bt_ref, at_ref, v_ref, bonus_ref, g_ref, gl_ref):
    i = pl.program_id(1)
    x = x_ref[...]
    tm, d = x.shape
    prev_row = jnp.where(i == 0, 0.0, xp_ref[7:8, :])
    row = _iota((tm, d), 0)
    xprev = jnp.where(row == 0, prev_row, pltpu.roll(x, 1, 0))
    xx = xprev - x
    mu = mu_ref[...]

    xr, xw, xk, xv, xa, xg = [(x + xx * mu[j:j + 1, :]).astype(BF16) for j in range(6)]
    hw = jnp.tanh(_dot(xw, w1_ref[...])).astype(BF16)
    ha = _dot(xa, a1_ref[...]).astype(BF16)
    hg = jax.nn.sigmoid(_dot(xg, g1_ref[...])).astype(BF16)

    ones_bd = _head_block_ones()
    tr = _iota((MXU_DIM, MXU_DIM), 0)
    tc = _iota((MXU_DIM, MXU_DIM), 1)
    tri = jnp.where((tr // WKV_L == tc // WKV_L) & (tc <= tr), 1.0, 0.0).astype(BF16)

    def col_group(c0):
        cl = slice(c0, c0 + MXU_DIM)
        r = _dot(xr, wr_ref[:, cl])
        k = _dot(xk, wk_ref[:, cl])
        v = _dot(xv, wv_ref[:, cl])
        wraw = w0_ref[:, cl] + _dot(hw, w2_ref[:, cl])
        a = jax.nn.sigmoid(a0_ref[:, cl] + _dot(ha, a2_ref[:, cl]))
        g = _dot(hg, g2_ref[:, cl])
        yield
        ld = -math.exp(-0.5) * jax.nn.sigmoid(wraw)
        kk = k * kk_ref[:, cl]
        kk = kk * jnp.minimum(lax.rsqrt(_dot(kk * kk, ones_bd)), 1e12)
        k = k * (1.0 + (a - 1.0) * ka_ref[:, cl])
        bonus = _dot(r * k * rk_ref[:, cl], ones_bd) * v
        ld_hi = ld.astype(BF16)
        ld_lo = (ld - ld_hi.astype(F32)).astype(BF16)
        cs = jnp.concatenate(
            [jnp.dot(tri, ld_hi[p:p + MXU_DIM, :], preferred_element_type=F32)
             + jnp.dot(tri, ld_lo[p:p + MXU_DIM, :], preferred_element_type=F32)
             for p in range(0, tm, MXU_DIM)], axis=0)
        yield
        igam = jnp.exp(-cs)
        rt_ref[:, cl] = (r * jnp.exp(cs)).astype(BF16)
        kt_ref[:, cl] = (k * igam).astype(BF16)
        bt_ref[:, cl] = (kk * a * igam).astype(BF16)
        at_ref[:, cl] = (-kk * jnp.exp(cs - ld)).astype(BF16)
        v_ref[:, cl] = v.astype(BF16)
        bonus_ref[:, cl] = bonus.astype(BF16)
        g_ref[:, cl] = g.astype(BF16)
        gl_ref[:, cl] = jnp.exp(cs.reshape(tm // WKV_L, WKV_L, MXU_DIM)[:, WKV_L - 1, :])

    _run_staggered([col_group(c0) for c0 in range(0, d, MXU_DIM)])


def _rwkv_proj(x, p, tm=PROJ_ROWS):
    bsz, t, d = x.shape
    nt = t // tm
    row_spec = pl.BlockSpec((None, tm, d), lambda b, i: (b, i, 0))
    prev_spec = pl.BlockSpec((None, 8, d), lambda b, i: (b, jnp.maximum(i * (tm // 8) - 1, 0), 0))

    def full(arr):
        nd = arr.ndim
        return pl.BlockSpec(arr.shape, lambda b, i: (0,) * nd, pipeline_mode=pl.Buffered(1))

    weights = [p["mu"], p["w_r"], p["w_k"], p["w_v"], p["w0"], p["w1"], p["w2"], p["a0"], p["a1"], p["a2"],
               p["g1"], p["g2"], p["k_k"], p["k_a"], p["r_k"]]
    out_bf = jax.ShapeDtypeStruct((bsz, t, d), BF16)
    out_shape = [out_bf] * 7 + [jax.ShapeDtypeStruct((bsz, t // WKV_L, d), F32)]
    out_specs = [row_spec] * 7 + [pl.BlockSpec((None, tm // WKV_L, d), lambda b, i: (b, i, 0))]
    return pl.pallas_call(
        _rwkv_proj_kernel,
        grid=(bsz, nt),
        in_specs=[row_spec, prev_spec] + [full(w) for w in weights],
        out_specs=out_specs,
        out_shape=out_shape,
        compiler_params=pltpu.CompilerParams(dimension_semantics=("parallel", "parallel"),
                                             vmem_limit_bytes=VMEM_LIMIT),
        name="rwkv_proj",
    )(x, x, *weights)


def _wkv_kernel(rt_ref, kt_ref, bt_ref, at_ref, v_ref, gl_ref, y_ref, st_ref, t_scr, rb_scr, x1_scr):
    tb, d = rt_ref.shape
    L, W = WKV_L, MXU_DIM
    npack = d // W
    row = _iota((L, W), 0)
    sidx = _iota((L, W), 1) % L
    strict = sidx < row
    incl = sidx <= row
    eye = sidx == row
    couple = {b: (row // (2 * b) == sidx // (2 * b)) & ((row // b) % 2 == 1) & ((sidx // b) % 2 == 0)
              for b in (1, 2, 4, 8, 16, 32)}
    r2 = _iota((W, W), 0)
    c2 = _iota((W, W), 1)
    bd = (r2 // L) == (c2 // L)
    diag = r2 == c2

    def stack(xb):
        return jnp.where(bd, jnp.concatenate([xb] * PACK, axis=0), jnp.zeros((), xb.dtype))

    @pl.when(pl.program_id(1) == 0)
    def _():
        st_ref[...] = jnp.zeros_like(st_ref)

    dyn0 = jnp.minimum(pl.program_id(0), 0)

    def rows(c):
        return pl.ds(pl.multiple_of(c * L, L), L)

    def prepare(c, pk, slot):
        slot = slot + dyn0
        sl = rows(c)
        ln = slice(pk * W, (pk + 1) * W)
        rt = rt_ref[sl, ln]
        kt = kt_ref[sl, ln]
        bt = bt_ref[sl, ln]
        at = at_ref[sl, ln]
        v = v_ref[sl, ln]
        gmat = _dot_nt(jnp.concatenate([at, rt], axis=0),
                       jnp.concatenate([stack(bt), stack(kt)], axis=0))
        yield
        ab = gmat[:L, :W]
        ak = gmat[:L, W:]
        rb = gmat[L:, :W]
        rk = gmat[L:, W:]
        tm_ = jnp.where(eye, 1.0, 0.0) + jnp.where(couple[1], ab, 0.0)
        res = _dot(jnp.concatenate([jnp.where(strict, ak, 0.0), jnp.where(incl, rk, 0.0)], axis=0), stack(v))
        yield
        for b in (2, 4, 8, 16, 32):
            ct = _dot(jnp.where(couple[b], ab, 0.0), stack(tm_.astype(BF16)))
            yield
            tm_ = tm_ + _dot(tm_, stack(ct.astype(BF16)))
            yield
        t_scr[slot, pk] = tm_.astype(BF16)
        rb_scr[slot, pk] = jnp.where(incl, rb, 0.0).astype(BF16)
        x1_scr[slot, pk] = res

    def advance(c, pk, slot):
        slot = slot + dyn0
        sl = rows(c)
        ln = slice(pk * W, (pk + 1) * W)
        tm_ = t_scr[slot, pk]
        rbi = rb_scr[slot, pk]
        x1r = x1_scr[slot, pk]
        rt = rt_ref[sl, ln]
        kt = kt_ref[sl, ln]
        bt = bt_ref[sl, ln]
        at = at_ref[sl, ln]
        v = v_ref[sl, ln]
        gl = gl_ref[pl.ds(c, 1), ln]
        st = st_ref[pk]
        ra = _dot(jnp.concatenate([rt, at], axis=0), st)
        yield
        ub = _dot(tm_, stack((x1r[:L] + ra[L:]).astype(BF16))).astype(BF16)
        yield
        yr = _dot(rbi, stack(ub))
        bh = (bt.astype(F32) * gl).astype(BF16)
        kh = (kt.astype(F32) * gl).astype(BF16)
        upd = _dot_tn(jnp.concatenate([bh, kh], axis=0), jnp.concatenate([ub, v], axis=0))
        yield
        glcol = jnp.sum(jnp.where(diag, gl, 0.0), axis=1, keepdims=True)
        st_ref[pk] = glcol * st + jnp.where(bd, upd, 0.0)
        y_ref[sl, ln] = (ra[:L] + yr + x1r[L:]).astype(y_ref.dtype)

    def run(*gens):
        for _ in itertools.zip_longest(*gens):
            pass

    npair = tb // (2 * L)
    packs = range(npack)

    def advance2(c, s0, s1, pk):
        return itertools.chain(advance(c, pk, s0), advance(c + 1, pk, s1))

    run(*([prepare(0, pk, 0) for pk in packs] + [prepare(1, pk, 1) for pk in packs]))

    def body(i, carry):
        c = 2 * i
        s0 = c % 4
        run(*([advance2(c, s0, s0 + 1, pk) for pk in packs]
              + [prepare(c + 2, pk, (s0 + 2) % 4) for pk in packs]
              + [prepare(c + 3, pk, (s0 + 3) % 4) for pk in packs]))
        return carry

    lax.fori_loop(0, npair - 1, body, 0, unroll=True)
    c_last = 2 * (npair - 1)
    run(*[advance2(c_last, c_last % 4, c_last % 4 + 1, pk) for pk in packs])


def _wkv(rt, kt, bt, at, v, gl, tb=WKV_TIME_BLOCK):
    bsz, t, d = rt.shape
    npack = d // MXU_DIM
    seq_spec = pl.BlockSpec((None, tb, d), lambda b, i: (b, i, 0))
    gl_spec = pl.BlockSpec((None, tb // WKV_L, d), lambda b, i: (b, i, 0))
    return pl.pallas_call(
        _wkv_kernel,
        grid=(bsz, t // tb),
        in_specs=[seq_spec] * 5 + [gl_spec],
        out_specs=seq_spec,
        out_shape=jax.ShapeDtypeStruct((bsz, t, d), BF16),
        scratch_shapes=[pltpu.VMEM((npack, MXU_DIM, MXU_DIM), F32),
                        pltpu.VMEM((4, npack, WKV_L, MXU_DIM), BF16),
                        pltpu.VMEM((4, npack, WKV_L, MXU_DIM), BF16),
                        pltpu.VMEM((4, npack, 2 * WKV_L, MXU_DIM), F32)],
        compiler_params=pltpu.CompilerParams(dimension_semantics=("parallel", "arbitrary"),
                                             vmem_limit_bytes=VMEM_LIMIT),
        name="wkv7",
    )(rt, kt, bt, at, v, gl)


def _post_block_kernel(*refs, rwkv, project, sub_rows):
    it = iter(refs)
    x_ref = next(it)
    y_ref = next(it)
    if rwkv:
        bonus_ref, g_ref, lnxg_ref, lnxb_ref = next(it), next(it), next(it), next(it)
    wo_ref, ln_ref, w1_ref, w2_ref = next(it), next(it), next(it), next(it)
    if project:
        wq_ref, wkv_ref = next(it), next(it)
    out_ref = next(it)
    if project:
        q_ref, k_ref, v_ref = next(it), next(it), next(it)

    tm, d = x_ref.shape
    ln = ln_ref[...]
    ones_bd = _head_block_ones() if rwkv else None

    def sub_tile(rs):
        x = x_ref[rs, :]
        if rwkv:
            y = y_ref[rs, :].astype(F32)
            mean = _head_sum_bcast(y, ones_bd) * (1.0 / HEAD)
            yc = y - mean
            var = _head_sum_bcast(yc * yc, ones_bd) * (1.0 / HEAD)
            yn = yc * lax.rsqrt(var + GN_EPS) * lnxg_ref[...] + lnxb_ref[...]
            mixed = ((yn + bonus_ref[rs, :].astype(F32)) * g_ref[rs, :].astype(F32)).astype(BF16)
        else:
            mixed = y_ref[rs, :]
        h = _dot(mixed, wo_ref[...])
        yield
        x1 = _layer_norm(ALPHA * x + h, ln[0:1, :], ln[1:2, :], LN_EPS)
        yield
        hid = jnp.maximum(_dot(x1, w1_ref[...]), 0.0)
        h2 = _dot(hid * hid, w2_ref[...])
        yield
        x2 = _layer_norm(ALPHA * x1 + h2, ln[2:3, :], ln[3:4, :], LN_EPS)
        out_ref[rs, :] = x2
        yield
        if project:
            x2b = x2.astype(BF16)
            q_ref[rs, :] = (_dot(x2b, wq_ref[...]) * (HEAD ** -0.5 * LOG2E)).astype(BF16)
            kv = _dot(x2b, wkv_ref[...])
            k_ref[rs, :] = kv[:, :d].astype(BF16)
            v_ref[rs, :] = kv[:, d:].astype(BF16)

    _run_staggered([sub_tile(slice(r, r + sub_rows)) for r in range(0, tm, sub_rows)])


def _post_block(x, y, rw, w_o, ln, w1, w2, layer, proj, sub_rows=POST_SUB_ROWS):
    bsz, t, d = x.shape
    rwkv = rw is not None
    project = proj is not None
    tm = 2 * sub_rows
    row_spec = pl.BlockSpec((None, tm, d), lambda b, i: (b, i, 0))

    def full(arr):
        nd = arr.ndim
        return pl.BlockSpec(arr.shape, lambda b, i: (0,) * nd, pipeline_mode=pl.Buffered(1))

    def slab(arr):
        return pl.BlockSpec((None,) + arr.shape[1:], lambda b, i: (layer, 0, 0), pipeline_mode=pl.Buffered(1))

    args = [x, y]
    specs = [row_spec, row_spec]
    if rwkv:
        args += [rw["bonus"], rw["g"], rw["lnx_g"], rw["lnx_b"]]
        specs += [row_spec, row_spec, full(rw["lnx_g"]), full(rw["lnx_b"])]
    args += [w_o, ln, w1, w2]
    specs += [full(w_o), full(ln), slab(w1), slab(w2)]
    out_shape = [jax.ShapeDtypeStruct((bsz, t, d), F32)]
    out_specs = [row_spec]
    if project:
        args += [proj["w_q"], proj["w_kv"]]
        specs += [full(proj["w_q"]), full(proj["w_kv"])]
        out_shape += [jax.ShapeDtypeStruct((bsz, t, d), BF16)] * 3
        out_specs += [row_spec] * 3
    return pl.pallas_call(
        functools.partial(_post_block_kernel, rwkv=rwkv, project=project, sub_rows=sub_rows),
        grid=(bsz, t // tm),
        in_specs=specs,
        out_specs=out_specs,
        out_shape=out_shape,
        compiler_params=pltpu.CompilerParams(dimension_semantics=("parallel", "parallel"),
                                             vmem_limit_bytes=VMEM_LIMIT),
        name="post_rwkv" if rwkv else "post_attn",
    )(*args)


def _t5_bucket_np(rel):
    nb = REL_BUCKETS // 2
    max_exact = nb // 2
    n = np.abs(rel)
    thresholds = [int(math.ceil(max_exact * (REL_MAX_DIST / max_exact) ** (m / (nb - max_exact)) - 1e-9))
                  for m in range(1, nb - max_exact)]
    large = max_exact + sum((n >= th).astype(np.int64) for th in thresholds)
    return np.where(rel > 0, nb, 0) + np.where(n < max_exact, n, np.minimum(large, nb - 1))


def _bias_bucket_tables():
    kl = np.arange(KEY_TILE)[:, None]
    qpos = np.arange(2 * Q_BLOCK)[None, :]
    out = np.zeros((2, KEY_TILE, 2 * Q_BLOCK), np.int32)
    out[0] = _t5_bucket_np(kl - KEY_TILE - qpos)
    allowed = (kl // ATT_CHUNK) <= (qpos // ATT_CHUNK)
    out[1] = np.where(allowed, _t5_bucket_np(kl - qpos), -1)
    return out


def _bias_table_kernel(bucket_ref, relb_ref, out_ref):
    h = pl.program_id(0)
    bucket = bucket_ref[...]
    acc = jnp.where(bucket < 0, NEG_BIG, 0.0).astype(F32)
    for b in range(REL_BUCKETS):
        acc = jnp.where(bucket == b, relb_ref[b, h], acc)
    out_ref[...] = jnp.where(bucket < 0, NEG_BIG, acc * LOG2E)


def _bias_tables(rel_bias):
    nh = rel_bias.shape[1]
    buckets = jnp.asarray(_bias_bucket_tables())
    return pl.pallas_call(
        _bias_table_kernel,
        grid=(nh,),
        in_specs=[pl.BlockSpec(buckets.shape, lambda h: (0, 0, 0)),
                  pl.BlockSpec(memory_space=pltpu.SMEM)],
        out_specs=pl.BlockSpec((None,) + buckets.shape, lambda h: (h, 0, 0, 0)),
        out_shape=jax.ShapeDtypeStruct((nh,) + buckets.shape, F32),
        name="t5_bias_tables",
    )(buckets, rel_bias)


def _diff_attn_kernel(q_ref, k_ref, v_ref, tab_ref, relb_ref, lam_ref, sg_ref, o_ref, s_scr, p_scr, ve_scr, *,
                      lambda_init):
    t = q_ref.shape[0]
    qb, kt_, dh = Q_BLOCK, KEY_TILE, HEAD
    dv = 2 * dh
    nhead = q_ref.shape[1] // dv
    h0 = pl.program_id(1) * nhead
    lam = lam_ref[...]
    lam_full = (jnp.exp(jnp.sum(lam[0:1, :] * lam[1:2, :], axis=-1, keepdims=True))
                - jnp.exp(jnp.sum(lam[2:3, :] * lam[3:4, :], axis=-1, keepdims=True)) + lambda_init)
    qp = 2 * qb
    lane = _iota((qp, dv), 1)
    sg = sg_ref[...] * (1.0 - lambda_init)
    for hh in range(nhead):
        ve_scr[hh, :dv, :] = jnp.transpose(v_ref[:, hh * dv:(hh + 1) * dv].astype(F32)).astype(BF16)
        ve_scr[hh, dv:, :] = jnp.ones((ve_scr.shape[1] - dv, t), BF16)

    npairs = t // kt_
    col_max = {}
    dyn0 = jnp.minimum(pl.program_id(0), 0)

    def logits(idx, hh, jl):
        slot = idx % 2 + dyn0
        hs = slice(hh * dv, (hh + 1) * dv)
        c_far = relb_ref[REL_BUCKETS // 2 - 1, h0 + hh] * LOG2E
        q = q_ref[jl * qp:(jl + 1) * qp, hs]
        zero = jnp.zeros((), q.dtype)
        qm = jnp.concatenate([jnp.where(lane < dh, q, zero), jnp.where(lane >= dh, q, zero)], axis=0)
        mx_far = jnp.full((1, 2 * qp), NEG_BIG, F32)
        mx_near = jnp.full((1, 2 * qp), NEG_BIG, F32)
        for j in range(jl + 1):
            ks = slice(j * kt_, (j + 1) * kt_)
            s = _dot_nt(k_ref[ks, hs], qm)
            if j >= jl - 1:
                tab = tab_ref[hh, 1 if j == jl else 0]
                s = s + jnp.concatenate([tab, tab], axis=1)
                mx_near = jnp.maximum(mx_near, jnp.max(s, axis=0, keepdims=True))
            else:
                mx_far = jnp.maximum(mx_far, jnp.max(s, axis=0, keepdims=True))
            s_scr[slot, ks, :] = s
            yield
        col_max[idx] = (jnp.maximum(mx_near, mx_far + c_far), c_far)

    def outputs(idx, hh, jl):
        slot = idx % 2 + dyn0
        m, c_far = col_max.pop(idx)
        m_far = m - c_far
        for j in range(jl + 1):
            ks = slice(j * kt_, (j + 1) * kt_)
            p_scr[ks, :] = jnp.exp2(s_scr[slot, ks, :] - (m if j >= jl - 1 else m_far)).astype(BF16)
            yield
        kend = (jl + 1) * kt_
        acc = jnp.dot(ve_scr[hh, :, :kend], p_scr[:kend, :], preferred_element_type=F32)
        pv = acc[:dv, :] / acc[dv:dv + 1, :]
        o = pv[:, :qp] - lam_full * pv[:, qp:]
        o = o * lax.rsqrt(jnp.mean(o * o, axis=0, keepdims=True) + SUBLN_EPS)
        o_ref[jl * qp:(jl + 1) * qp, hh * dv:(hh + 1) * dv] = jnp.transpose(o * sg).astype(o_ref.dtype)

    def flash(hh, jl):
        hs = slice(hh * dv, (hh + 1) * dv)
        c_far = relb_ref[REL_BUCKETS // 2 - 1, h0 + hh] * LOG2E
        q = q_ref[jl * qp:(jl + 1) * qp, hs]
        zero = jnp.zeros((), q.dtype)
        qm = jnp.concatenate([jnp.where(lane < dh, q, zero), jnp.where(lane >= dh, q, zero)], axis=0)
        m = jnp.full((1, 2 * qp), NEG_BIG, F32)
        acc = jnp.zeros((ve_scr.shape[1], 2 * qp), F32)
        for j in reversed(range(jl + 1)):
            ks = slice(j * kt_, (j + 1) * kt_)
            s = _dot_nt(k_ref[ks, hs], qm)
            if j >= jl - 1:
                tab = tab_ref[hh, 1 if j == jl else 0]
                s = s + jnp.concatenate([tab, tab], axis=1)
            else:
                s = s + c_far
            m_new = jnp.maximum(m, jnp.max(s, axis=0, keepdims=True))
            p = jnp.exp2(s - m_new).astype(BF16)
            acc = acc * jnp.exp2(m - m_new) + jnp.dot(ve_scr[hh, :, ks], p, preferred_element_type=F32)
            m = m_new
            yield
        pv = acc[:dv, :] / acc[dv:dv + 1, :]
        o = pv[:, :qp] - lam_full * pv[:, qp:]
        o = o * lax.rsqrt(jnp.mean(o * o, axis=0, keepdims=True) + SUBLN_EPS)
        o_ref[jl * qp:(jl + 1) * qp, hh * dv:(hh + 1) * dv] = jnp.transpose(o * sg).astype(o_ref.dtype)

    chains = [(hh, jl) for jl in reversed(range(npairs)) for hh in range(nhead)]
    for g0 in range(0, len(chains), 4):
        for _ in itertools.zip_longest(*[flash(hh, jl) for hh, jl in chains[g0:g0 + 4]]):
            pass


def _diff_attn(q, k, v, tabs, rel_bias, lam, subln_g, lambda_init, heads_per_step=ATTN_HEADS_PER_STEP):
    bsz, t, d = q.shape
    nh = d // (2 * HEAD)
    hps = heads_per_step
    seq_spec = pl.BlockSpec((None, t, hps * 2 * HEAD), lambda b, h: (b, 0, h))
    return pl.pallas_call(
        functools.partial(_diff_attn_kernel, lambda_init=lambda_init),
        grid=(bsz, nh // hps),
        in_specs=[seq_spec, seq_spec, seq_spec,
                  pl.BlockSpec((hps,) + tabs.shape[1:], lambda b, h: (h, 0, 0, 0)),
                  pl.BlockSpec(memory_space=pltpu.SMEM),
                  pl.BlockSpec(lam.shape, lambda b, h: (0, 0)),
                  pl.BlockSpec(subln_g.shape, lambda b, h: (0, 0))],
        out_specs=seq_spec,
        out_shape=jax.ShapeDtypeStruct((bsz, t, d), BF16),
        scratch_shapes=[pltpu.VMEM((2, t, 4 * Q_BLOCK), F32),
                        pltpu.VMEM((t, 4 * Q_BLOCK), BF16),
                        pltpu.VMEM((hps, 2 * HEAD + 16, t), BF16)],
        compiler_params=pltpu.CompilerParams(dimension_semantics=("parallel", "parallel"),
                                             vmem_limit_bytes=VMEM_LIMIT),
        name="diff_attn",
    )(q, k, v, tabs, rel_bias, lam, subln_g)


def kernel(x, a_mu, a_w_r, a_w_k, a_w_v, a_w_o, a_w0, a_w1, a_w2, a_a0, a_a1, a_a2, a_g1, a_g2, a_k_k, a_k_a,
           a_r_k, a_lnx_g, a_lnx_b, b_w_kv, b_w_q, b_lam, b_subln_g, b_w_o, rel_bias, mlp_w1, mlp_w2, ln_g, ln_b):
    bsz, t, d = x.shape
    assert d % MXU_DIM == 0 and t % WKV_TIME_BLOCK == 0 and t % PROJ_ROWS == 0 and t % (2 * POST_SUB_ROWS) == 0
    bf = lambda w: w.astype(BF16)
    row = lambda w: w.reshape(1, d)

    p = dict(mu=a_mu[0], w_r=bf(a_w_r[0]), w_k=bf(a_w_k[0]), w_v=bf(a_w_v[0]), w0=row(a_w0[0]),
             w1=bf(a_w1[0]), w2=bf(a_w2[0]), a0=row(a_a0[0]), a1=bf(a_a1[0]), a2=bf(a_a2[0]),
             g1=bf(a_g1[0]), g2=bf(a_g2[0]), k_k=row(a_k_k[0]), k_a=row(a_k_a[0]), r_k=row(a_r_k[0]))
    rt, kt, bt, at, v, bonus, g, gl = _rwkv_proj(x, p)
    y = _wkv(rt, kt, bt, at, v, gl)
    ln0 = jnp.stack([ln_g[0, 0], ln_b[0, 0], ln_g[0, 1], ln_b[0, 1]])
    w1_all, w2_all = bf(mlp_w1), bf(mlp_w2)
    x, q, k_sh, v_sh = _post_block(
        x, y, dict(bonus=bonus, g=g, lnx_g=row(a_lnx_g[0]), lnx_b=row(a_lnx_b[0])),
        bf(a_w_o[0]), ln0, w1_all, w2_all, 0, dict(w_q=bf(b_w_q[0]), w_kv=bf(b_w_kv)))

    lambda_init = 0.8 - 0.6 * math.exp(-0.3 * 1)
    tabs = _bias_tables(rel_bias)
    o = _diff_attn(q, k_sh, v_sh, tabs, rel_bias, b_lam[0], b_subln_g[0].reshape(2 * HEAD, 1), lambda_init)
    ln1 = jnp.stack([ln_g[1, 0], ln_b[1, 0], ln_g[1, 1], ln_b[1, 1]])
    (x,) = _post_block(x, o, None, bf(b_w_o[0]), ln1, w1_all, w2_all, 1, None)
    return x
```
